```python
import jax
import jax.numpy as jnp
from jax import lax
import numpy as np

D_MODEL = 1024
BATCH = 8
SEQ = 2048
DEPTH = 2
DEC_BATCH = 128
DEC_SEQ = 1
PAST_LEN = 16384
PAGE_SIZE = 128

HEAD_SIZE = 64
N_HEADS = D_MODEL // HEAD_SIZE
DECAY_LORA = 64
AAA_LORA = 64
GATE_LORA = 128
CONV_WIDTH = 31
D_FF = ((8 * D_MODEL // 3 + 127) // 128) * 128
N_EXPERTS = 8
TOP_K = 2
D_FF_EXPERT = D_FF // 2
N_RWKV = (DEPTH + 1) // 2
N_CONV = DEPTH // 2
RMS_EPS = 1e-6
LN_EPS = 1e-5
GN_EPS = 64e-5

kernel_name = "rwkv7_conformer_conv_hybrid_adaln_step"


def rms_norm(x, g):
    xf = x.astype(jnp.float32)
    y = xf * lax.rsqrt(jnp.mean(xf * xf, axis=-1, keepdims=True) + RMS_EPS)
    return (y * g.astype(jnp.float32)).astype(x.dtype)


def adaln(c, w, b, n):
    p = jax.nn.silu(c) @ w + b
    return jnp.split(p, n, axis=-1)


def modulate(h, shift, scale):
    return h * (1 + scale[:, None, :]) + shift[:, None, :]


def rwkv7_time_mix(h, shift_prev, s0, mu, w_r, w_k, w_v, w_o, w0, w1, w2, a0, a1, a2,
                   g1, g2, k_k, k_a, r_k, lnx_g, lnx_b):
    B, T, D = h.shape
    f32 = jnp.float32
    h_prev = jnp.concatenate([shift_prev[:, None, :].astype(h.dtype), h[:, :-1]], axis=1)
    xx = h_prev - h
    xr, xw, xk, xv, xa, xg = [h + xx * mu[i] for i in range(6)]
    r = xr @ w_r
    k = xk @ w_k
    v = xv @ w_v
    w_raw = (w0 + jnp.tanh(xw @ w1) @ w2).astype(f32)
    log_decay = -jnp.exp(-jax.nn.softplus(-w_raw) - 0.5)
    a = jax.nn.sigmoid((a0 + (xa @ a1) @ a2).astype(f32))
    g = jax.nn.sigmoid(xg @ g1) @ g2

    def heads(t):
        return t.astype(f32).reshape(B, T, N_HEADS, HEAD_SIZE)

    kk = heads(k * k_k)
    kk = kk / jnp.maximum(jnp.sqrt(jnp.sum(kk * kk, axis=-1, keepdims=True)), 1e-12)
    k_eff = k.astype(f32) * (1 + (a - 1) * k_a.astype(f32))
    rh, kh, vh, ah = heads(r), heads(k_eff), heads(v), heads(a)
    decay = jnp.exp(heads(log_decay))

    def tm(t):
        return jnp.swapaxes(t, 0, 1)

    def step(S, inp):
        r_t, w_t, k_t, v_t, a_t, b_t = inp
        sa = jnp.einsum('bhij,bhj->bhi', S, a_t)
        S = S * w_t[:, :, None, :] + sa[..., None] * b_t[:, :, None, :] + v_t[..., None] * k_t[:, :, None, :]
        y = jnp.einsum('bhij,bhj->bhi', S, r_t)
        return S, y

    s_final, ys = lax.scan(step, s0.astype(f32),
                           (tm(rh), tm(decay), tm(kh), tm(vh), tm(-kk), tm(kk * ah)))
    y = jnp.swapaxes(ys, 0, 1)
    mean = jnp.mean(y, axis=-1, keepdims=True)
    var = jnp.mean(jnp.square(y - mean), axis=-1, keepdims=True)
    yn = ((y - mean) * lax.rsqrt(var + GN_EPS)).reshape(B, T, D) * lnx_g.astype(f32) + lnx_b.astype(f32)
    bonus = (jnp.sum(rh * kh * r_k.astype(f32), axis=-1, keepdims=True) * vh).reshape(B, T, D)
    out = ((yn + bonus).astype(h.dtype) * g) @ w_o
    return out, h[:, -1], s_final.astype(h.dtype)


def conformer_conv(h, buf, pw1_w, pw1_b, dw_w, dw_b, ln_g, ln_b, pw2_w, pw2_b):
    D = h.shape[-1]
    u = h @ pw1_w + pw1_b
    u = u[..., :D] * jax.nn.sigmoid(u[..., D:])
    full = jnp.concatenate([buf.astype(u.dtype), u], axis=1)
    z = lax.conv_general_dilated(full, dw_w[:, None, :].astype(u.dtype), window_strides=(1,),
                                 padding='VALID', dimension_numbers=('NWC', 'WIO', 'NWC'),
                                 feature_group_count=D) + dw_b
    zf = z.astype(jnp.float32)
    mean = jnp.mean(zf, axis=-1, keepdims=True)
    var = jnp.mean(jnp.square(zf - mean), axis=-1, keepdims=True)
    zn = ((zf - mean) * lax.rsqrt(var + LN_EPS) * ln_g.astype(jnp.float32) + ln_b.astype(jnp.float32)).astype(h.dtype)
    out = jax.nn.silu(zn) @ pw2_w + pw2_b
    return out, full[:, -(CONV_WIDTH - 1):, :]


def swiglu(h, w1, w3, w2):
    return (jax.nn.silu(h @ w1) * (h @ w3)) @ w2


def moe_swiglu(h, router_w, router_b, w1, w3, w2):
    logits = (h @ router_w + router_b).astype(jnp.float32)
    top_v, top_i = lax.top_k(logits, TOP_K)
    gates = jax.nn.softmax(top_v, axis=-1)
    combine = jnp.sum(jax.nn.one_hot(top_i, N_EXPERTS, dtype=jnp.float32) * gates[..., None], axis=-2)
    y = jnp.zeros_like(h)
    for e in range(N_EXPERTS):
        y = y + combine[..., e:e + 1].astype(h.dtype) * swiglu(h, w1[e], w3[e], w2[e])
    return y


def run_trunk(x, c, wkv0, shift0, conv0, shared, rw, cv, ffn, moe, fin):
    ada_w, ada_b, norm_mix_g, norm_ffn_g = shared
    wkv_out, shift_out, conv_out = [], [], []
    for i in range(DEPTH):
        j = i // 2
        sh1, sc1, gt1, sh2, sc2, gt2 = adaln(c, ada_w[i], ada_b[i], 6)
        h = modulate(rms_norm(x, norm_mix_g[i]), sh1, sc1)
        if i % 2 == 0:
            o, last, s = rwkv7_time_mix(h, shift0[j], wkv0[j], *[p[j] for p in rw])
            shift_out.append(last)
            wkv_out.append(s)
        else:
            o, buf = conformer_conv(h, conv0[j], *[p[j] for p in cv])
            conv_out.append(buf)
        x = x + gt1[:, None, :] * o
        h = modulate(rms_norm(x, norm_ffn_g[i]), sh2, sc2)
        if i % 2 == 0:
            f = swiglu(h, *[p[j] for p in ffn])
        else:
            f = moe_swiglu(h, *[p[j] for p in moe])
        x = x + gt2[:, None, :] * f
    final_g, final_ada_w, final_ada_b = fin
    sh, sc = adaln(c, final_ada_w, final_ada_b, 2)
    y = modulate(rms_norm(x, final_g), sh, sc)
    return y, jnp.stack(wkv_out), jnp.stack(shift_out), jnp.stack(conv_out)


def setup_inputs(seed: int = 0) -> dict:
    key = jax.random.key(seed)
    keys = jax.random.split(key, 64)
    cnt = [0]

    def nxt():
        cnt[0] += 1
        return keys[cnt[0] - 1]

    def nrm(shape, scale):
        return jax.random.normal(nxt(), shape, jnp.float32) * scale

    def unif(shape, lo, hi):
        return jax.random.uniform(nxt(), shape, jnp.float32, lo, hi)

    D = D_MODEL
    dinv = D ** -0.5
    A, C = N_RWKV, N_CONV
    inp = {}
    inp['x_prompt'] = nrm((BATCH, SEQ, D), 1.0)
    inp['x_sample'] = nrm((DEC_BATCH, DEC_SEQ, D), 1.0)
    inp['state_wkv'] = nrm((A, DEC_BATCH, N_HEADS, HEAD_SIZE, HEAD_SIZE), 0.3)
    inp['state_shift'] = nrm((A, DEC_BATCH, D), 1.0)
    inp['state_conv'] = nrm((C, DEC_BATCH, CONV_WIDTH - 1, D), 0.5)
    inp['c_prompt'] = nrm((BATCH, D), 1.0)
    inp['c_sample'] = nrm((DEC_BATCH, D), 1.0)
    inp['ada_w'] = nrm((DEPTH, D, 6 * D), 0.5 * dinv)
    inp['ada_b'] = nrm((DEPTH, 6 * D), 0.02)
    inp['norm_mix_g'] = 1.0 + nrm((DEPTH, D), 0.05)
    inp['norm_ffn_g'] = 1.0 + nrm((DEPTH, D), 0.05)
    inp['rw_mu'] = unif((A, 6, D), 0.0, 1.0)
    inp['rw_wr'] = nrm((A, D, D), dinv)
    inp['rw_wk'] = nrm((A, D, D), dinv)
    inp['rw_wv'] = nrm((A, D, D), dinv)
    inp['rw_wo'] = nrm((A, D, D), dinv)
    inp['rw_w0'] = unif((A, D), -5.5, -0.5)
    inp['rw_w1'] = nrm((A, D, DECAY_LORA), dinv)
    inp['rw_w2'] = nrm((A, DECAY_LORA, D), 0.1)
    inp['rw_a0'] = nrm((A, D), 0.1)
    inp['rw_a1'] = nrm((A, D, AAA_LORA), dinv)
    inp['rw_a2'] = nrm((A, AAA_LORA, D), 0.5 * AAA_LORA ** -0.5)
    inp['rw_g1'] = nrm((A, D, GATE_LORA), dinv)
    inp['rw_g2'] = nrm((A, GATE_LORA, D), GATE_LORA ** -0.5)
    inp['rw_kk'] = 0.85 + nrm((A, D), 0.05)
    inp['rw_ka'] = 1.0 + nrm((A, D), 0.05)
    inp['rw_rk'] = nrm((A, N_HEADS, HEAD_SIZE), 0.1)
    inp['rw_lnx_g'] = 1.0 + nrm((A, D), 0.05)
    inp['rw_lnx_b'] = nrm((A, D), 0.01)
    inp['cv_pw1_w'] = nrm((C, D, 2 * D), dinv)
    inp['cv_pw1_b'] = nrm((C, 2 * D), 0.01)
    inp['cv_dw_w'] = nrm((C, CONV_WIDTH, D), CONV_WIDTH ** -0.5)
    inp['cv_dw_b'] = nrm((C, D), 0.01)
    inp['cv_ln_g'] = 1.0 + nrm((C, D), 0.05)
    inp['cv_ln_b'] = nrm((C, D), 0.01)
    inp['cv_pw2_w'] = nrm((C, D, D), dinv)
    inp['cv_pw2_b'] = nrm((C, D), 0.01)
    inp['ffn_w1'] = nrm((A, D, D_FF), dinv)
    inp['ffn_w3'] = nrm((A, D, D_FF), dinv)
    inp['ffn_w2'] = nrm((A, D_FF, D), D_FF ** -0.5)
    inp['moe_router_w'] = nrm((C, D, N_EXPERTS), dinv)
    inp['moe_router_b'] = nrm((C, N_EXPERTS), 0.01)
    inp['moe_w1'] = nrm((C, N_EXPERTS, D, D_FF_EXPERT), dinv)
    inp['moe_w3'] = nrm((C, N_EXPERTS, D, D_FF_EXPERT), dinv)
    inp['moe_w2'] = nrm((C, N_EXPERTS, D_FF_EXPERT, D), D_FF_EXPERT ** -0.5)
    inp['final_g'] = 1.0 + nrm((D,), 0.05)
    inp['final_ada_w'] = nrm((D, 2 * D), 0.5 * dinv)
    inp['final_ada_b'] = nrm((2 * D,), 0.02)
    return inp


def reference(x_prompt, x_sample, state_wkv, state_shift, state_conv, c_prompt, c_sample,
              ada_w, ada_b, norm_mix_g, norm_ffn_g,
              rw_mu, rw_wr, rw_wk, rw_wv, rw_wo, rw_w0, rw_w1, rw_w2, rw_a0, rw_a1, rw_a2,
              rw_g1, rw_g2, rw_kk, rw_ka, rw_rk, rw_lnx_g, rw_lnx_b,
              cv_pw1_w, cv_pw1_b, cv_dw_w, cv_dw_b, cv_ln_g, cv_ln_b, cv_pw2_w, cv_pw2_b,
              ffn_w1, ffn_w3, ffn_w2,
              moe_router_w, moe_router_b, moe_w1, moe_w3, moe_w2,
              final_g, final_ada_w, final_ada_b):
    shared = (ada_w, ada_b, norm_mix_g, norm_ffn_g)
    rw = (rw_mu, rw_wr, rw_wk, rw_wv, rw_wo, rw_w0, rw_w1, rw_w2, rw_a0, rw_a1, rw_a2,
          rw_g1, rw_g2, rw_kk, rw_ka, rw_rk, rw_lnx_g, rw_lnx_b)
    cv = (cv_pw1_w, cv_pw1_b, cv_dw_w, cv_dw_b, cv_ln_g, cv_ln_b, cv_pw2_w, cv_pw2_b)
    ffn = (ffn_w1, ffn_w3, ffn_w2)
    moe = (moe_router_w, moe_router_b, moe_w1, moe_w3, moe_w2)
    fin = (final_g, final_ada_w, final_ada_b)
    b = x_prompt.shape[0]
    dt = x_prompt.dtype
    zero_wkv = jnp.zeros((N_RWKV, b, N_HEADS, HEAD_SIZE, HEAD_SIZE), dt)
    zero_shift = jnp.zeros((N_RWKV, b, D_MODEL), dt)
    zero_conv = jnp.zeros((N_CONV, b, CONV_WIDTH - 1, D_MODEL), dt)
    y_prompt, wkv_prompt, shift_prompt, conv_prompt = run_trunk(
        x_prompt, c_prompt, zero_wkv, zero_shift, zero_conv, shared, rw, cv, ffn, moe, fin)
    y_sample, wkv_sample, shift_sample, conv_sample = run_trunk(
        x_sample, c_sample, state_wkv, state_shift, state_conv, shared, rw, cv, ffn, moe, fin)
    return (y_prompt, y_sample, wkv_prompt, shift_prompt, conv_prompt, wkv_sample, shift_sample, conv_sample)
```

```python
import functools

import jax
import jax.numpy as jnp
from jax import lax
from jax.experimental import pallas as pl
from jax.experimental.pallas import tpu as pltpu

F32 = jnp.float32
BF16 = jnp.bfloat16

HEAD_SIZE = 64
RMS_EPS = 1e-6
LN_EPS = 1e-5
GN_EPS = 64e-5
TOP_K = 2

V7X_LANES = 128
V7X_SUBLANES = 8
V7X_VMEM_BYTES = 64 * 1024 * 1024
VMEM_LIMIT_BYTES = V7X_VMEM_BYTES - 8 * 1024 * 1024


def _cparams(*sem):
    return pltpu.CompilerParams(dimension_semantics=sem, vmem_limit_bytes=VMEM_LIMIT_BYTES)


def _tile(n, pref, mult):
    if n <= pref:
        return n
    t = (pref // mult) * mult
    while t >= mult:
        if n % t == 0:
            return t
        t -= mult
    return n


def _bdot(a, b):
    return jnp.dot(a.astype(BF16), b.astype(BF16), preferred_element_type=F32)


def _split_bf16(x):
    hi = x.astype(BF16)
    lo = (x - hi.astype(F32)).astype(BF16)
    return hi, lo


def _dot3(a, b):
    a_hi, a_lo = _split_bf16(a)
    b_hi, b_lo = _split_bf16(b)
    d = functools.partial(jnp.dot, preferred_element_type=F32)
    return d(a_hi, b_hi) + d(a_hi, b_lo) + d(a_lo, b_hi)


def _rms_mod(x, g, shift, scale):
    ms = jnp.mean(x * x, axis=-1, keepdims=True)
    y = x * lax.rsqrt(ms + RMS_EPS) * g
    return y * (1.0 + scale) + shift


def _const_spec(shape):
    nd = len(shape)
    return pl.BlockSpec(shape, lambda *_: (0,) * nd)


def _ada_kernel(c_ref, w_ref, b_ref, o_ref):
    s = jax.nn.silu(c_ref[...])
    o_ref[...] = _dot3(s, w_ref[...]) + b_ref[...]


def _ada(c, w, b):
    rows, d = c.shape
    n = w.shape[1]
    tn = _tile(n, 1024, V7X_LANES)
    return pl.pallas_call(
        _ada_kernel,
        out_shape=jax.ShapeDtypeStruct((rows, n), F32),
        grid=(n // tn,),
        in_specs=[
            pl.BlockSpec((rows, d), lambda j: (0, 0)),
            pl.BlockSpec((d, tn), lambda j: (0, j)),
            pl.BlockSpec((1, tn), lambda j: (0, j)),
        ],
        out_specs=pl.BlockSpec((rows, tn), lambda j: (0, j)),
        compiler_params=_cparams("arbitrary"),
        name="ada",
    )(c, w, b.reshape(1, n))


def _row_spec(tm, d):
    return pl.BlockSpec((None, tm, d), lambda s, t: (s, t, 0))


def _mod_spec(arr, tm):
    d = arr.shape[-1]
    if arr.shape[1] == 1:
        return pl.BlockSpec((None, 1, d), lambda s, t: (s, 0, 0))
    return pl.BlockSpec((None, tm, d), lambda s, t: (s, t, 0))


def _w2_spec(shape):
    return pl.BlockSpec(shape, lambda s, t: (0, 0))


def _pre0_kernel(sequential, *refs):
    if sequential:
        (x_ref, g_ref, sh_ref, sc_ref, mu_ref, wr_ref, wk_ref, wv_ref, w1_ref, w2_ref, a1_ref,
         a2_ref, g1_ref, g2_ref, w0_ref, a0_ref,
         r_out, k_out, v_out, w_out, a_out, g_out, h_out, carry) = refs
        hprev_ref = None
    else:
        (x_ref, g_ref, sh_ref, sc_ref, hprev_ref, mu_ref, wr_ref, wk_ref, wv_ref, w1_ref, w2_ref,
         a1_ref, a2_ref, g1_ref, g2_ref, w0_ref, a0_ref,
         r_out, k_out, v_out, w_out, a_out, g_out, h_out) = refs
    h = _rms_mod(x_ref[...], g_ref[...], sh_ref[...], sc_ref[...])
    tm = h.shape[0]
    if sequential:
        @pl.when(pl.program_id(1) == 0)
        def _():
            carry[...] = jnp.zeros_like(carry)
        row = lax.broadcasted_iota(jnp.int32, h.shape, 0)
        h_prev = jnp.where(row == 0, carry[...], pltpu.roll(h, 1, 0))
        carry[...] = h[tm - 1:tm, :]
        h_out[...] = h[tm - 1:tm, :]
    else:
        h_prev = hprev_ref[...]
        h_out[...] = h
    xx = h_prev - h
    mu = mu_ref[...]
    xr, xw, xk, xv, xa, xg = [h + xx * mu[i:i + 1, :] for i in range(6)]
    r_out[...] = _bdot(xr, wr_ref[...])
    k_out[...] = _bdot(xk, wk_ref[...])
    v_out[...] = _bdot(xv, wv_ref[...])
    w_raw = w0_ref[...] + _bdot(jnp.tanh(_bdot(xw, w1_ref[...])), w2_ref[...])
    w_out[...] = jnp.exp(-jnp.exp(-0.5) * jax.nn.sigmoid(w_raw))
    a_out[...] = jax.nn.sigmoid(a0_ref[...] + _bdot(_bdot(xa, a1_ref[...]), a2_ref[...]))
    g_out[...] = _bdot(jax.nn.sigmoid(_bdot(xg, g1_ref[...])), g2_ref[...])


def _pre0(x, g, sh, sc, h_prev, p, tm_pref=256):
    s_, r_, d = x.shape
    tm = _tile(r_, tm_pref, V7X_SUBLANES)
    sequential = h_prev is None
    weights = [p["mu"], p["wr"], p["wk"], p["wv"], p["w1"], p["w2"], p["a1"], p["a2"], p["g1"],
               p["g2"], p["w0"], p["a0"]]
    ins = [x, g, sh, sc] + ([] if sequential else [h_prev]) + weights
    in_specs = [_row_spec(tm, d), _w2_spec(g.shape), _mod_spec(sh, tm), _mod_spec(sc, tm)]
    if not sequential:
        in_specs.append(_row_spec(tm, d))
    in_specs += [_w2_spec(w.shape) for w in weights]
    big = jax.ShapeDtypeStruct((s_, r_, d), F32)
    if sequential:
        h_shape = jax.ShapeDtypeStruct((s_, 1, d), F32)
        h_spec = pl.BlockSpec((None, 1, d), lambda s, t: (s, 0, 0))
        scratch = [pltpu.VMEM((1, d), F32)]
    else:
        h_shape, h_spec, scratch = big, _row_spec(tm, d), []
    return pl.pallas_call(
        functools.partial(_pre0_kernel, sequential),
        out_shape=[big] * 6 + [h_shape],
        grid=(s_, r_ // tm),
        in_specs=in_specs,
        out_specs=[_row_spec(tm, d)] * 6 + [h_spec],
        scratch_shapes=scratch,
        compiler_params=_cparams("arbitrary", "arbitrary"),
        name="pre0",
    )(*ins)


def _scan_kernel(zero_init, tc_len, *refs):
    if zero_init:
        (r_ref, w_ref, k_ref, v_ref, a_ref, kk_ref, ka_ref, rk_ref, lg_ref, lb_ref,
         z_ref, sT_ref, s_scr, ab_scr) = refs
        s0_ref = None
    else:
        (r_ref, w_ref, k_ref, v_ref, a_ref, s0_ref, kk_ref, ka_ref, rk_ref, lg_ref, lb_ref,
         z_ref, sT_ref, s_scr, ab_scr) = refs
    n = HEAD_SIZE

    @pl.when(pl.program_id(1) == 0)
    def _():
        if zero_init:
            s_scr[...] = jnp.zeros_like(s_scr)
        else:
            s_scr[...] = s0_ref[...]

    kk_t, ka_t, rk_t, lg_t, lb_t = kk_ref[...], ka_ref[...], rk_ref[...], lg_ref[...], lb_ref[...]

    def step(t, carry):
        k = k_ref[t]
        a = a_ref[t]
        v = v_ref[t]
        r = r_ref[t]
        kk = k * kk_t
        nrm = jnp.sqrt(jnp.sum(kk * kk, axis=0, keepdims=True))
        kk = kk / jnp.maximum(nrm, 1e-12)
        k_eff = k * (1.0 + (a - 1.0) * ka_t)
        ab_scr[0] = -kk
        ab_scr[1] = kk * a
        ab_scr[2] = k_eff
        sa = s_scr[0] * ab_scr[0, 0:1, :]
        for j in range(1, n):
            sa = sa + s_scr[j] * ab_scr[0, j:j + 1, :]
        y = None
        for j in range(n):
            s_new = (s_scr[j] * w_ref[t, j:j + 1, :] + sa * ab_scr[1, j:j + 1, :]
                     + v * ab_scr[2, j:j + 1, :])
            s_scr[j] = s_new
            term = s_new * r_ref[t, j:j + 1, :]
            y = term if y is None else y + term
        mean = jnp.mean(y, axis=0, keepdims=True)
        dlt = y - mean
        var = jnp.mean(dlt * dlt, axis=0, keepdims=True)
        yn = dlt * lax.rsqrt(var + GN_EPS) * lg_t + lb_t
        bonus = jnp.sum(r * k_eff * rk_t, axis=0, keepdims=True) * v
        z_ref[t] = yn + bonus
        return carry

    lax.fori_loop(0, tc_len, step, 0)

    @pl.when(pl.program_id(1) == pl.num_programs(1) - 1)
    def _():
        sT_ref[...] = s_scr[...]


def _scan(r, w, k, v, a, s0, consts, tc_pref=64):
    t_len, n, inst = r.shape
    tc = _tile(t_len, tc_pref, 1)
    groups = inst // V7X_LANES
    op_spec = pl.BlockSpec((tc, n, V7X_LANES), lambda g, c: (c, 0, g))
    st_spec = pl.BlockSpec((n, n, V7X_LANES), lambda g, c: (0, 0, g))
    c_spec = pl.BlockSpec((n, V7X_LANES), lambda g, c: (0, 0))
    zero_init = s0 is None
    ins = [r, w, k, v, a] + ([] if zero_init else [s0]) + list(consts)
    in_specs = [op_spec] * 5 + ([] if zero_init else [st_spec]) + [c_spec] * len(consts)
    return pl.pallas_call(
        functools.partial(_scan_kernel, zero_init, tc),
        out_shape=[jax.ShapeDtypeStruct((t_len, n, inst), F32),
                   jax.ShapeDtypeStruct((n, n, inst), F32)],
        grid=(groups, t_len // tc),
        in_specs=in_specs,
        out_specs=[op_spec, st_spec],
        scratch_shapes=[pltpu.VMEM((n, n, V7X_LANES), F32), pltpu.VMEM((3, n, V7X_LANES), F32)],
        compiler_params=_cparams("arbitrary", "arbitrary"),
        name="wkv_scan",
    )(*ins)


def _to_lanes(x, heads):
    s_, r_, d = x.shape
    return jnp.transpose(x.reshape(s_, r_, heads, HEAD_SIZE), (1, 3, 0, 2)).reshape(
        r_, HEAD_SIZE, s_ * heads)


def _from_lanes(z, s_, heads):
    t_len = z.shape[0]
    return jnp.transpose(z.reshape(t_len, HEAD_SIZE, s_, heads), (2, 0, 3, 1)).reshape(
        s_, t_len, heads * HEAD_SIZE)


def _post0_kernel(z_ref, g_ref, x_ref, wo_ref, gt_ref, ng_ref, sh_ref, sc_ref, x1_out, h2_out):
    o = _bdot(z_ref[...] * g_ref[...], wo_ref[...])
    x1 = x_ref[...] + gt_ref[...] * o
    x1_out[...] = x1
    h2_out[...] = _rms_mod(x1, ng_ref[...], sh_ref[...], sc_ref[...]).astype(BF16)


def _post0(z, g, x, wo, gt, ng, sh, sc, tm_pref=512):
    s_, r_, d = x.shape
    tm = _tile(r_, tm_pref, V7X_SUBLANES)
    rs = _row_spec(tm, d)
    return pl.pallas_call(
        _post0_kernel,
        out_shape=[jax.ShapeDtypeStruct((s_, r_, d), F32), jax.ShapeDtypeStruct((s_, r_, d), BF16)],
        grid=(s_, r_ // tm),
        in_specs=[rs, rs, rs, _w2_spec(wo.shape), _mod_spec(gt, tm), _w2_spec(ng.shape),
                  _mod_spec(sh, tm), _mod_spec(sc, tm)],
        out_specs=[rs, rs],
        compiler_params=_cparams("arbitrary", "arbitrary"),
        name="post0",
    )(z, g, x, wo, gt, ng, sh, sc)


def _ffn_kernel(h_ref, x_ref, w1_ref, w3_ref, w2_ref, gt_ref, x2_out, acc):
    f = pl.program_id(2)

    @pl.when(f == 0)
    def _():
        acc[...] = jnp.zeros_like(acc)
    h = h_ref[...]
    a = jax.nn.silu(jnp.dot(h, w1_ref[...], preferred_element_type=F32))
    b = jnp.dot(h, w3_ref[...], preferred_element_type=F32)
    acc[...] += _bdot(a * b, w2_ref[...])

    @pl.when(f == pl.num_programs(2) - 1)
    def _():
        x2_out[...] = x_ref[...] + gt_ref[...] * acc[...]


def _ffn(h2, x1, w1, w3, w2, gt, tm_pref=512, tf_pref=1408):
    s_, r_, d = x1.shape
    ff = w1.shape[1]
    tm = _tile(r_, tm_pref, V7X_SUBLANES)
    tf = _tile(ff, tf_pref, V7X_LANES)
    rs = pl.BlockSpec((None, tm, d), lambda s, t, f: (s, t, 0))
    if gt.shape[1] == 1:
        gs = pl.BlockSpec((None, 1, d), lambda s, t, f: (s, 0, 0))
    else:
        gs = rs
    return pl.pallas_call(
        _ffn_kernel,
        out_shape=jax.ShapeDtypeStruct((s_, r_, d), F32),
        grid=(s_, r_ // tm, ff // tf),
        in_specs=[rs, rs,
                  pl.BlockSpec((d, tf), lambda s, t, f: (0, f)),
                  pl.BlockSpec((d, tf), lambda s, t, f: (0, f)),
                  pl.BlockSpec((tf, d), lambda s, t, f: (f, 0)),
                  gs],
        out_specs=rs,
        scratch_shapes=[pltpu.VMEM((tm, d), F32)],
        compiler_params=_cparams("arbitrary", "arbitrary", "arbitrary"),
        name="ffn",
    )(h2, x1, w1, w3, w2, gt)


def _glu_kernel(x_ref, ng_ref, sh_ref, sc_ref, pw_ref, pb_ref, u_out):
    d = x_ref.shape[-1]
    h = _rms_mod(x_ref[...], ng_ref[...], sh_ref[...], sc_ref[...])
    u = _bdot(h, pw_ref[...]) + pb_ref[...]
    u_out[...] = u[:, :d] * jax.nn.sigmoid(u[:, d:])


def _glu(x, ng, sh, sc, pw, pb, tm_pref=512):
    s_, r_, d = x.shape
    tm = _tile(r_, tm_pref, V7X_SUBLANES)
    rs = _row_spec(tm, d)
    return pl.pallas_call(
        _glu_kernel,
        out_shape=jax.ShapeDtypeStruct((s_, r_, d), F32),
        grid=(s_, r_ // tm),
        in_specs=[rs, _w2_spec(ng.shape), _mod_spec(sh, tm), _mod_spec(sc, tm),
                  _w2_spec(pw.shape), _w2_spec(pb.shape)],
        out_specs=rs,
        compiler_params=_cparams("arbitrary", "arbitrary"),
        name="glu",
    )(x, ng, sh, sc, pw, pb)


def _conv_tail(z, x, lng, lnb, pw2, pb2, gt, ng, sh, sc, rw, rb):
    mean = jnp.mean(z, axis=-1, keepdims=True)
    dlt = z - mean
    var = jnp.mean(dlt * dlt, axis=-1, keepdims=True)
    zn = dlt * lax.rsqrt(var + LN_EPS) * lng + lnb
    o = _bdot(jax.nn.silu(zn), pw2) + pb2
    x3 = x + gt * o
    h3 = _rms_mod(x3, ng, sh, sc)
    logits = _dot3(h3, rw) + rb
    n_exp = rb.shape[-1]
    lane = lax.broadcasted_iota(jnp.int32, logits.shape, 1)
    neg = jnp.float32(-jnp.inf)
    lg = logits
    v1 = jnp.max(lg, axis=-1, keepdims=True)
    i1 = jnp.min(jnp.where(lg == v1, lane, n_exp), axis=-1, keepdims=True)
    lg2 = jnp.where(lane == i1, neg, lg)
    v2 = jnp.max(lg2, axis=-1, keepdims=True)
    i2 = jnp.min(jnp.where(lg2 == v2, lane, n_exp), axis=-1, keepdims=True)
    e2 = jnp.exp(v2 - v1)
    den = 1.0 + e2
    comb = jnp.where(lane == i1, 1.0 / den, 0.0) + jnp.where(lane == i2, e2 / den, 0.0)
    return x3, h3, comb


def _conv_seq_kernel(width, u_ref, x_ref, dw_ref, db_ref, lng_ref, lnb_ref, pw2_ref, pb2_ref,
                     gt_ref, ng_ref, sh_ref, sc_ref, rw_ref, rb_ref,
                     x3_out, h3_out, comb_out, cst_out, ext):
    tm = u_ref.shape[0]
    halo = ext.shape[0] - tm
    t = pl.program_id(1)

    @pl.when(t == 0)
    def _():
        ext[0:halo, :] = jnp.zeros((halo, ext.shape[1]), F32)

    @pl.when(t > 0)
    def _():
        ext[0:halo, :] = ext[tm:tm + halo, :]
    ext[halo:halo + tm, :] = u_ref[...]
    off = halo - (width - 1)
    z = ext[off:off + tm, :] * dw_ref[0:1, :]
    for wi in range(1, width):
        z = z + ext[off + wi:off + wi + tm, :] * dw_ref[wi:wi + 1, :]
    z = z + db_ref[...]
    x3, h3, comb = _conv_tail(z, x_ref[...], lng_ref[...], lnb_ref[...], pw2_ref[...], pb2_ref[...],
                              gt_ref[...], ng_ref[...], sh_ref[...], sc_ref[...], rw_ref[...],
                              rb_ref[...])
    x3_out[...] = x3
    h3_out[...] = h3.astype(BF16)
    comb_out[...] = comb
    cst_out[...] = ext[halo + tm - (width - 1):halo + tm, :]


def _conv_seq(u, x, cw, gt, ng, sh, sc, rw, rb, tm_pref=512):
    s_, r_, d = x.shape
    width = cw["dw"].shape[0]
    tm = _tile(r_, tm_pref, V7X_SUBLANES)
    halo = -(-(width - 1) // V7X_SUBLANES) * V7X_SUBLANES
    assert tm >= halo and r_ >= width - 1
    rs = _row_spec(tm, d)
    ws = [cw["dw"], cw["db"], cw["lng"], cw["lnb"], cw["pw2"], cw["pb2"]]
    n_lane = rw.shape[1]
    return pl.pallas_call(
        functools.partial(_conv_seq_kernel, width),
        out_shape=[jax.ShapeDtypeStruct((s_, r_, d), F32), jax.ShapeDtypeStruct((s_, r_, d), BF16),
                   jax.ShapeDtypeStruct((s_, r_, n_lane), F32),
                   jax.ShapeDtypeStruct((s_, width - 1, d), F32)],
        grid=(s_, r_ // tm),
        in_specs=[rs, rs] + [_w2_spec(w.shape) for w in ws]
        + [_mod_spec(gt, tm), _w2_spec(ng.shape), _mod_spec(sh, tm), _mod_spec(sc, tm),
           _w2_spec(rw.shape), _w2_spec(rb.shape)],
        out_specs=[rs, rs, pl.BlockSpec((None, tm, n_lane), lambda s, t: (s, t, 0)),
                   pl.BlockSpec((None, width - 1, d), lambda s, t: (s, 0, 0))],
        scratch_shapes=[pltpu.VMEM((tm + halo, d), F32)],
        compiler_params=_cparams("arbitrary", "arbitrary"),
        name="conv_seq",
    )(u, x, *ws, gt, ng, sh, sc, rw, rb)


def _conv_step_kernel(width, u_ref, x_ref, buf_ref, dw_ref, db_ref, lng_ref, lnb_ref, pw2_ref,
                      pb2_ref, gt_ref, ng_ref, sh_ref, sc_ref, rw_ref, rb_ref,
                      x3_out, h3_out, comb_out, cst_out):
    u = u_ref[...]
    z = u * dw_ref[width - 1:width, :] + db_ref[...]
    for wi in range(width - 1):
        z = z + buf_ref[:, wi, :] * dw_ref[wi:wi + 1, :]
    x3, h3, comb = _conv_tail(z, x_ref[...], lng_ref[...], lnb_ref[...], pw2_ref[...], pb2_ref[...],
                              gt_ref[...], ng_ref[...], sh_ref[...], sc_ref[...], rw_ref[...],
                              rb_ref[...])
    x3_out[...] = x3
    h3_out[...] = h3.astype(BF16)
    comb_out[...] = comb
    for wi in range(width - 2):
        cst_out[:, wi, :] = buf_ref[:, wi + 1, :]
    cst_out[:, width - 2, :] = u


def _conv_step(u, x, buf, cw, gt, ng, sh, sc, rw, rb, tm_pref=32):
    s_, r_, d = x.shape
    width = cw["dw"].shape[0]
    tm = _tile(r_, tm_pref, V7X_SUBLANES)
    rs = _row_spec(tm, d)
    bs = pl.BlockSpec((tm, width - 1, d), lambda s, t: (t, 0, 0))
    ws = [cw["dw"], cw["db"], cw["lng"], cw["lnb"], cw["pw2"], cw["pb2"]]
    n_lane = rw.shape[1]
    return pl.pallas_call(
        functools.partial(_conv_step_kernel, width),
        out_shape=[jax.ShapeDtypeStruct((s_, r_, d), F32), jax.ShapeDtypeStruct((s_, r_, d), BF16),
                   jax.ShapeDtypeStruct((s_, r_, n_lane), F32),
                   jax.ShapeDtypeStruct((r_, width - 1, d), F32)],
        grid=(s_, r_ // tm),
        in_specs=[rs, rs, bs] + [_w2_spec(w.shape) for w in ws]
        + [_mod_spec(gt, tm), _w2_spec(ng.shape), _mod_spec(sh, tm), _mod_spec(sc, tm),
           _w2_spec(rw.shape), _w2_spec(rb.shape)],
        out_specs=[rs, rs, pl.BlockSpec((None, tm, n_lane), lambda s, t: (s, t, 0)), bs],
        compiler_params=_cparams("arbitrary", "arbitrary"),
        name="conv_step",
    )(u, x, buf, *ws, gt, ng, sh, sc, rw, rb)


def _moe_kernel(h_ref, comb_ref, x_ref, w1_ref, w3_ref, w2_ref, gt_ref, fg_ref, fsh_ref, fsc_ref,
                y_out, acc):
    e = pl.program_id(2)

    @pl.when(e == 0)
    def _():
        acc[...] = jnp.zeros_like(acc)
    h = h_ref[...]
    a = jax.nn.silu(jnp.dot(h, w1_ref[...], preferred_element_type=F32))
    b = jnp.dot(h, w3_ref[...], preferred_element_type=F32)
    f = _bdot(a * b, w2_ref[...])
    comb = comb_ref[...]
    lane = lax.broadcasted_iota(jnp.int32, comb.shape, 1)
    col = jnp.sum(jnp.where(lane == e, comb, 0.0), axis=-1, keepdims=True)
    acc[...] += col * f

    @pl.when(e == pl.num_programs(2) - 1)
    def _():
        x4 = x_ref[...] + gt_ref[...] * acc[...]
        y_out[...] = _rms_mod(x4, fg_ref[...], fsh_ref[...], fsc_ref[...])


def _moe(h3, comb, x3, w1, w3, w2, gt, fg, fsh, fsc, tm_pref=1024):
    s_, r_, d = x3.shape
    n_exp, _, fe = w1.shape
    tm = _tile(r_, tm_pref, V7X_SUBLANES)
    n_lane = comb.shape[-1]
    rs = pl.BlockSpec((None, tm, d), lambda s, t, e: (s, t, 0))

    def ms(arr):
        if arr.shape[1] == 1:
            return pl.BlockSpec((None, 1, d), lambda s, t, e: (s, 0, 0))
        return rs
    return pl.pallas_call(
        _moe_kernel,
        out_shape=jax.ShapeDtypeStruct((s_, r_, d), F32),
        grid=(s_, r_ // tm, n_exp),
        in_specs=[rs, pl.BlockSpec((None, tm, n_lane), lambda s, t, e: (s, t, 0)), rs,
                  pl.BlockSpec((None, d, fe), lambda s, t, e: (e, 0, 0)),
                  pl.BlockSpec((None, d, fe), lambda s, t, e: (e, 0, 0)),
                  pl.BlockSpec((None, fe, d), lambda s, t, e: (e, 0, 0)),
                  ms(gt), pl.BlockSpec(fg.shape, lambda s, t, e: (0, 0)), ms(fsh), ms(fsc)],
        out_specs=rs,
        scratch_shapes=[pltpu.VMEM((tm, d), F32)],
        compiler_params=_cparams("arbitrary", "arbitrary", "arbitrary"),
        name="moe",
    )(h3, comb, x3, w1, w3, w2, gt, fg, fsh, fsc)


def _trunk(x, mods0, mods1, modf, wkv0, shift0, conv0, P):
    s_, r_, d = x.shape
    heads = d // HEAD_SIZE
    sh1, sc1, gt1, sh2, sc2, gt2 = mods0
    r, k, v, w, a, g, h_last = _pre0(x, P["nmix0"], sh1, sc1, shift0, P["rw"])
    if wkv0 is None:
        ops = [_to_lanes(t, heads) for t in (r, w, k, v, a)]
        z, s_fin = _scan(*ops, None, P["scan_consts"])
        z = _from_lanes(z, s_, heads)
        n_seq = s_
    else:
        ops = [_to_lanes(t.reshape(r_, 1, d), heads) for t in (r, w, k, v, a)]
        z, s_fin = _scan(*ops, wkv0, P["scan_consts"])
        z = _from_lanes(z, r_, heads).reshape(1, r_, d)
        n_seq = r_
    wkv_out = jnp.transpose(s_fin.reshape(HEAD_SIZE, HEAD_SIZE, n_seq, heads), (2, 3, 1, 0))
    x1, h2 = _post0(z, g, x, P["wo"], gt1, P["nffn0"], sh2, sc2)
    x2 = _ffn(h2, x1, P["ffn_w1"], P["ffn_w3"], P["ffn_w2"], gt2)
    sh1, sc1, gt1, sh2, sc2, gt2 = mods1
    u = _glu(x2, P["nmix1"], sh1, sc1, P["pw1"], P["pb1"])
    if conv0 is None:
        x3, h3, comb, conv_out = _conv_seq(u, x2, P["cv"], gt1, P["nffn1"], sh2, sc2,
                                           P["router_w"], P["router_b"])
    else:
        x3, h3, comb, conv_out = _conv_step(u, x2, conv0, P["cv"], gt1, P["nffn1"], sh2, sc2,
                                            P["router_w"], P["router_b"])
    fsh, fsc = modf
    y = _moe(h3, comb, x3, P["moe_w1"], P["moe_w3"], P["moe_w2"], gt2, P["final_g"], fsh, fsc)
    return y, wkv_out, h_last, conv_out


def kernel(x_prompt, x_sample, state_wkv, state_shift, state_conv, c_prompt, c_sample, ada_w, ada_b, norm_mix_g, norm_ffn_g, rw_mu, rw_wr, rw_wk, rw_wv, rw_wo, rw_w0, rw_w1, rw_w2, rw_a0, rw_a1, rw_a2, rw_g1, rw_g2, rw_kk, rw_ka, rw_rk, rw_lnx_g, rw_lnx_b, cv_pw1_w, cv_pw1_b, cv_dw_w, cv_dw_b, cv_ln_g, cv_ln_b, cv_pw2_w, cv_pw2_b, ffn_w1, ffn_w3, ffn_w2, moe_router_w, moe_router_b, moe_w1, moe_w3, moe_w2, final_g, final_ada_w, final_ada_b):
    nb, seq, d = x_prompt.shape
    db = x_sample.shape[0]
    heads = d // HEAD_SIZE
    n_exp = moe_router_w.shape[-1]
    assert (nb * heads) % V7X_LANES == 0 and (db * heads) % V7X_LANES == 0
    assert V7X_LANES % heads == 0 and x_sample.shape[1] == 1

    def row(vec):
        return vec.reshape(1, -1)

    def lane_tile(vec):
        t = vec.reshape(heads, HEAD_SIZE).T
        return jnp.tile(t, (1, V7X_LANES // heads))

    bf = lambda t: t.astype(BF16)
    P = {
        "nmix0": row(norm_mix_g[0]), "nffn0": row(norm_ffn_g[0]),
        "nmix1": row(norm_mix_g[1]), "nffn1": row(norm_ffn_g[1]),
        "rw": {"mu": rw_mu[0], "wr": bf(rw_wr[0]), "wk": bf(rw_wk[0]), "wv": bf(rw_wv[0]),
               "w1": bf(rw_w1[0]), "w2": bf(rw_w2[0]), "a1": bf(rw_a1[0]), "a2": bf(rw_a2[0]),
               "g1": bf(rw_g1[0]), "g2": bf(rw_g2[0]), "w0": row(rw_w0[0]), "a0": row(rw_a0[0])},
        "scan_consts": [lane_tile(rw_kk[0]), lane_tile(rw_ka[0]), lane_tile(rw_rk[0].reshape(-1)),
                        lane_tile(rw_lnx_g[0]), lane_tile(rw_lnx_b[0])],
        "wo": bf(rw_wo[0]),
        "ffn_w1": bf(ffn_w1[0]), "ffn_w3": bf(ffn_w3[0]), "ffn_w2": bf(ffn_w2[0]),
        "pw1": bf(cv_pw1_w[0]), "pb1": row(cv_pw1_b[0]),
        "cv": {"dw": cv_dw_w[0], "db": row(cv_dw_b[0]), "lng": row(cv_ln_g[0]),
               "lnb": row(cv_ln_b[0]), "pw2": bf(cv_pw2_w[0]), "pb2": row(cv_pw2_b[0])},
        "router_w": jnp.pad(moe_router_w[0], ((0, 0), (0, V7X_LANES - n_exp))),
        "router_b": jnp.pad(row(moe_router_b[0]), ((0, 0), (0, V7X_LANES - n_exp)),
                            constant_values=-jnp.inf),
        "moe_w1": bf(moe_w1[0]), "moe_w3": bf(moe_w3[0]), "moe_w2": bf(moe_w2[0]),
        "final_g": row(final_g),
    }

    c_all = jnp.concatenate([c_prompt, c_sample], axis=0)
    ada = [_ada(c_all, ada_w[i], ada_b[i]) for i in range(ada_w.shape[0])]
    ada_f = _ada(c_all, final_ada_w, final_ada_b)

    def mods(p, n, prompt):
        parts = jnp.split(p, n, axis=-1)
        if prompt:
            return [t[:nb].reshape(nb, 1, d) for t in parts]
        return [t[nb:].reshape(1, db, d) for t in parts]

    y_p, wkv_p, shift_p, conv_p = _trunk(
        x_prompt, mods(ada[0], 6, True), mods(ada[1], 6, True), mods(ada_f, 2, True),
        None, None, None, P)
    wkv0 = jnp.transpose(state_wkv[0], (3, 2, 0, 1)).reshape(HEAD_SIZE, HEAD_SIZE, db * heads)
    y_s, wkv_s, shift_s, conv_s = _trunk(
        x_sample.reshape(1, db, d), mods(ada[0], 6, False), mods(ada[1], 6, False),
        mods(ada_f, 2, False), wkv0, state_shift[0].reshape(1, db, d), state_conv[0], P)
    return (y_p, y_s.reshape(db, 1, d), wkv_p[None], shift_p.reshape(1, nb, d), conv_p[None],
            wkv_s[None], shift_s.reshape(1, db, d), conv_s[None])
```

```python
import functools

import jax
import jax.numpy as jnp
from jax import lax
from jax.experimental import pallas as pl
from jax.experimental.pallas import tpu as pltpu

F32 = jnp.float32
BF16 = jnp.bfloat16

HEAD_SIZE = 64
RMS_EPS = 1e-6
LN_EPS = 1e-5
GN_EPS = 64e-5
MIN_DECAY = 0.5452
SCAN_CHUNK = 64

V7X_LANES = 128
V7X_SUBLANES = 8
V7X_VMEM_BYTES = 64 * 1024 * 1024
VMEM_LIMIT_BYTES = V7X_VMEM_BYTES - 8 * 1024 * 1024


def _cparams(*sem):
    return pltpu.CompilerParams(dimension_semantics=sem, vmem_limit_bytes=VMEM_LIMIT_BYTES)


def _tile(n, pref, mult):
    if n <= pref:
        return n
    t = (pref // mult) * mult
    while t >= mult:
        if n % t == 0:
            return t
        t -= mult
    return n


def _bdot(a, b):
    return jnp.dot(a.astype(BF16), b.astype(BF16), preferred_element_type=F32)


def _split_bf16(x):
    hi = x.astype(BF16)
    lo = (x - hi.astype(F32)).astype(BF16)
    return hi, lo


def _dot3(a, b):
    a_hi, a_lo = _split_bf16(a)
    b_hi, b_lo = _split_bf16(b)
    d = functools.partial(jnp.dot, preferred_element_type=F32)
    return d(a_hi, b_hi) + d(a_hi, b_lo) + d(a_lo, b_hi)


def _rms_mod(x, g, shift, scale):
    ms = jnp.mean(x * x, axis=-1, keepdims=True)
    y = x * lax.rsqrt(ms + RMS_EPS) * g
    return y * (1.0 + scale) + shift


def _const_spec(shape):
    nd = len(shape)
    return pl.BlockSpec(shape, lambda *_: (0,) * nd)


def _ada_kernel(c_ref, w_ref, b_ref, o_ref):
    s = jax.nn.silu(c_ref[...])
    o_ref[...] = _dot3(s, w_ref[...]) + b_ref[...]


def _ada(c, w, b):
    rows, d = c.shape
    n = w.shape[1]
    tn = _tile(n, 1024, V7X_LANES)
    return pl.pallas_call(
        _ada_kernel,
        out_shape=jax.ShapeDtypeStruct((rows, n), F32),
        grid=(n // tn,),
        in_specs=[
            pl.BlockSpec((rows, d), lambda j: (0, 0)),
            pl.BlockSpec((d, tn), lambda j: (0, j)),
            pl.BlockSpec((1, tn), lambda j: (0, j)),
        ],
        out_specs=pl.BlockSpec((rows, tn), lambda j: (0, j)),
        compiler_params=_cparams("arbitrary"),
        name="ada",
    )(c, w, b.reshape(1, n))


def _row_spec(tm, d):
    return pl.BlockSpec((None, tm, d), lambda s, t: (s, t, 0))


def _mod_spec(arr, tm):
    d = arr.shape[-1]
    if arr.shape[1] == 1:
        return pl.BlockSpec((None, 1, d), lambda s, t: (s, 0, 0))
    return pl.BlockSpec((None, tm, d), lambda s, t: (s, t, 0))


def _w2_spec(shape):
    return pl.BlockSpec(shape, lambda s, t: (0, 0))


def _pre0_kernel(sequential, *refs):
    if sequential:
        (x_ref, g_ref, sh_ref, sc_ref, mu_ref, wr_ref, wk_ref, wv_ref, w1_ref, w2_ref, a1_ref,
         a2_ref, g1_ref, g2_ref, w0_ref, a0_ref,
         r_out, k_out, v_out, w_out, a_out, g_out, h_out, carry) = refs
        hprev_ref = None
    else:
        (x_ref, g_ref, sh_ref, sc_ref, hprev_ref, mu_ref, wr_ref, wk_ref, wv_ref, w1_ref, w2_ref,
         a1_ref, a2_ref, g1_ref, g2_ref, w0_ref, a0_ref,
         r_out, k_out, v_out, w_out, a_out, g_out, h_out) = refs
    h = _rms_mod(x_ref[...], g_ref[...], sh_ref[...], sc_ref[...])
    tm = h.shape[0]
    if sequential:
        @pl.when(pl.program_id(1) == 0)
        def _():
            carry[...] = jnp.zeros_like(carry)
        row = lax.broadcasted_iota(jnp.int32, h.shape, 0)
        h_prev = jnp.where(row == 0, carry[...], pltpu.roll(h, 1, 0))
        carry[...] = h[tm - 1:tm, :]
        h_out[...] = h[tm - 1:tm, :]
    else:
        h_prev = hprev_ref[...]
        h_out[...] = h
    xx = h_prev - h
    mu = mu_ref[...]
    xr, xw, xk, xv, xa, xg = [h + xx * mu[i:i + 1, :] for i in range(6)]
    r_out[...] = _bdot(xr, wr_ref[...])
    k_out[...] = _bdot(xk, wk_ref[...])
    v_out[...] = _bdot(xv, wv_ref[...])
    w_raw = w0_ref[...] + _bdot(jnp.tanh(_bdot(xw, w1_ref[...])), w2_ref[...])
    w_out[...] = jnp.exp(-jnp.exp(-0.5) * jax.nn.sigmoid(w_raw))
    a_out[...] = jax.nn.sigmoid(a0_ref[...] + _bdot(_bdot(xa, a1_ref[...]), a2_ref[...]))
    g_out[...] = _bdot(jax.nn.sigmoid(_bdot(xg, g1_ref[...])), g2_ref[...])


def _pre0(x, g, sh, sc, h_prev, p, tm_pref=256):
    s_, r_, d = x.shape
    tm = _tile(r_, tm_pref, V7X_SUBLANES)
    sequential = h_prev is None
    weights = [p["mu"], p["wr"], p["wk"], p["wv"], p["w1"], p["w2"], p["a1"], p["a2"], p["g1"],
               p["g2"], p["w0"], p["a0"]]
    ins = [x, g, sh, sc] + ([] if sequential else [h_prev]) + weights
    in_specs = [_row_spec(tm, d), _w2_spec(g.shape), _mod_spec(sh, tm), _mod_spec(sc, tm)]
    if not sequential:
        in_specs.append(_row_spec(tm, d))
    in_specs += [_w2_spec(w.shape) for w in weights]
    big = jax.ShapeDtypeStruct((s_, r_, d), F32)
    if sequential:
        h_shape = jax.ShapeDtypeStruct((s_, 1, d), F32)
        h_spec = pl.BlockSpec((None, 1, d), lambda s, t: (s, 0, 0))
        scratch = [pltpu.VMEM((1, d), F32)]
    else:
        h_shape, h_spec, scratch = big, _row_spec(tm, d), []
    return pl.pallas_call(
        functools.partial(_pre0_kernel, sequential),
        out_shape=[big] * 6 + [h_shape],
        grid=(s_, r_ // tm),
        in_specs=in_specs,
        out_specs=[_row_spec(tm, d)] * 6 + [h_spec],
        scratch_shapes=scratch,
        compiler_params=_cparams("arbitrary", "arbitrary"),
        name="pre0",
    )(*ins)


def _scan_kernel(zero_init, tc_len, *refs):
    if zero_init:
        (r_ref, w_ref, k_ref, v_ref, a_ref, kk_ref, ka_ref, rk_ref, lg_ref, lb_ref,
         z_ref, sT_ref, s_scr, ab_scr) = refs
        s0_ref = None
    else:
        (r_ref, w_ref, k_ref, v_ref, a_ref, s0_ref, kk_ref, ka_ref, rk_ref, lg_ref, lb_ref,
         z_ref, sT_ref, s_scr, ab_scr) = refs
    n = HEAD_SIZE

    @pl.when(pl.program_id(1) == 0)
    def _():
        if zero_init:
            s_scr[...] = jnp.zeros_like(s_scr)
        else:
            s_scr[...] = s0_ref[...]

    kk_t, ka_t, rk_t, lg_t, lb_t = kk_ref[...], ka_ref[...], rk_ref[...], lg_ref[...], lb_ref[...]

    def step(t, c_prev):
        k = k_ref[t]
        a = a_ref[t]
        v = v_ref[t]
        r = r_ref[t]
        kk = k * kk_t
        nrm = jnp.sqrt(jnp.sum(kk * kk, axis=0, keepdims=True))
        kk = kk / jnp.maximum(nrm, 1e-12)
        k_eff = k * (1.0 + (a - 1.0) * ka_t)
        c_new = c_prev * w_ref[t]
        c_inv = 1.0 / c_new
        ab_scr[0] = -kk * c_prev
        ab_scr[1] = kk * a * c_inv
        ab_scr[2] = k_eff * c_inv
        ab_scr[3] = r * c_new
        sa = s_scr[0] * ab_scr[0, 0:1, :]
        for j in range(1, n):
            sa = sa + s_scr[j] * ab_scr[0, j:j + 1, :]
        y = None
        for j in range(n):
            q_new = s_scr[j] + sa * ab_scr[1, j:j + 1, :] + v * ab_scr[2, j:j + 1, :]
            s_scr[j] = q_new
            term = q_new * ab_scr[3, j:j + 1, :]
            y = term if y is None else y + term
        mean = jnp.mean(y, axis=0, keepdims=True)
        dlt = y - mean
        var = jnp.mean(dlt * dlt, axis=0, keepdims=True)
        yn = dlt * lax.rsqrt(var + GN_EPS) * lg_t + lb_t
        bonus = jnp.sum(r * k_eff * rk_t, axis=0, keepdims=True) * v
        z_ref[t] = yn + bonus
        return c_new

    c_fin = lax.fori_loop(0, tc_len, step, jnp.ones((n, V7X_LANES), F32))
    ab_scr[0] = c_fin
    for j in range(n):
        s_scr[j] = s_scr[j] * ab_scr[0, j:j + 1, :]

    @pl.when(pl.program_id(1) == pl.num_programs(1) - 1)
    def _():
        sT_ref[...] = s_scr[...]


def _scan(r, w, k, v, a, s0, consts):
    t_len, n, inst = r.shape
    tc = _tile(t_len, SCAN_CHUNK, 1)
    assert MIN_DECAY ** tc > 1e-30
    groups = inst // V7X_LANES
    op_spec = pl.BlockSpec((tc, n, V7X_LANES), lambda g, c: (c, 0, g))
    st_spec = pl.BlockSpec((n, n, V7X_LANES), lambda g, c: (0, 0, g))
    c_spec = pl.BlockSpec((n, V7X_LANES), lambda g, c: (0, 0))
    zero_init = s0 is None
    ins = [r, w, k, v, a] + ([] if zero_init else [s0]) + list(consts)
    in_specs = [op_spec] * 5 + ([] if zero_init else [st_spec]) + [c_spec] * len(consts)
    return pl.pallas_call(
        functools.partial(_scan_kernel, zero_init, tc),
        out_shape=[jax.ShapeDtypeStruct((t_len, n, inst), F32),
                   jax.ShapeDtypeStruct((n, n, inst), F32)],
        grid=(groups, t_len // tc),
        in_specs=in_specs,
        out_specs=[op_spec, st_spec],
        scratch_shapes=[pltpu.VMEM((n, n, V7X_LANES), F32), pltpu.VMEM((4, n, V7X_LANES), F32)],
        compiler_params=_cparams("arbitrary", "arbitrary"),
        name="wkv_scan",
    )(*ins)


def _to_lanes(x, heads):
    s_, r_, d = x.shape
    return jnp.transpose(x.reshape(s_, r_, heads, HEAD_SIZE), (1, 3, 0, 2)).reshape(
        r_, HEAD_SIZE, s_ * heads)


def _from_lanes(z, s_, heads):
    t_len = z.shape[0]
    return jnp.transpose(z.reshape(t_len, HEAD_SIZE, s_, heads), (2, 0, 3, 1)).reshape(
        s_, t_len, heads * HEAD_SIZE)


def _post0_kernel(z_ref, g_ref, x_ref, wo_ref, gt_ref, ng_ref, sh_ref, sc_ref, x1_out, h2_out):
    o = _bdot(z_ref[...] * g_ref[...], wo_ref[...])
    x1 = x_ref[...] + gt_ref[...] * o
    x1_out[...] = x1
    h2_out[...] = _rms_mod(x1, ng_ref[...], sh_ref[...], sc_ref[...]).astype(BF16)


def _post0(z, g, x, wo, gt, ng, sh, sc, tm_pref=512):
    s_, r_, d = x.shape
    tm = _tile(r_, tm_pref, V7X_SUBLANES)
    rs = _row_spec(tm, d)
    return pl.pallas_call(
        _post0_kernel,
        out_shape=[jax.ShapeDtypeStruct((s_, r_, d), F32), jax.ShapeDtypeStruct((s_, r_, d), BF16)],
        grid=(s_, r_ // tm),
        in_specs=[rs, rs, rs, _w2_spec(wo.shape), _mod_spec(gt, tm), _w2_spec(ng.shape),
                  _mod_spec(sh, tm), _mod_spec(sc, tm)],
        out_specs=[rs, rs],
        compiler_params=_cparams("arbitrary", "arbitrary"),
        name="post0",
    )(z, g, x, wo, gt, ng, sh, sc)


def _ffn_kernel(h_ref, x_ref, w1_ref, w3_ref, w2_ref, gt_ref, x2_out, acc):
    f = pl.program_id(2)

    @pl.when(f == 0)
    def _():
        acc[...] = jnp.zeros_like(acc)
    h = h_ref[...]
    a = jax.nn.silu(jnp.dot(h, w1_ref[...], preferred_element_type=F32))
    b = jnp.dot(h, w3_ref[...], preferred_element_type=F32)
    acc[...] += _bdot(a * b, w2_ref[...])

    @pl.when(f == pl.num_programs(2) - 1)
    def _():
        x2_out[...] = x_ref[...] + gt_ref[...] * acc[...]


def _ffn(h2, x1, w1, w3, w2, gt, tm_pref=512, tf_pref=1408):
    s_, r_, d = x1.shape
    ff = w1.shape[1]
    tm = _tile(r_, tm_pref, V7X_SUBLANES)
    tf = _tile(ff, tf_pref, V7X_LANES)
    rs = pl.BlockSpec((None, tm, d), lambda s, t, f: (s, t, 0))
    if gt.shape[1] == 1:
        gs = pl.BlockSpec((None, 1, d), lambda s, t, f: (s, 0, 0))
    else:
        gs = rs
    return pl.pallas_call(
        _ffn_kernel,
        out_shape=jax.ShapeDtypeStruct((s_, r_, d), F32),
        grid=(s_, r_ // tm, ff // tf),
        in_specs=[rs, rs,
                  pl.BlockSpec((d, tf), lambda s, t, f: (0, f)),
                  pl.BlockSpec((d, tf), lambda s, t, f: (0, f)),
                  pl.BlockSpec((tf, d), lambda s, t, f: (f, 0)),
                  gs],
        out_specs=rs,
        scratch_shapes=[pltpu.VMEM((tm, d), F32)],
        compiler_params=_cparams("arbitrary", "arbitrary", "arbitrary"),
        name="ffn",
    )(h2, x1, w1, w3, w2, gt)


def _glu_kernel(x_ref, ng_ref, sh_ref, sc_ref, pw_ref, pb_ref, u_out):
    d = x_ref.shape[-1]
    h = _rms_mod(x_ref[...], ng_ref[...], sh_ref[...], sc_ref[...])
    u = _bdot(h, pw_ref[...]) + pb_ref[...]
    u_out[...] = u[:, :d] * jax.nn.sigmoid(u[:, d:])


def _glu(x, ng, sh, sc, pw, pb, tm_pref=512):
    s_, r_, d = x.shape
    tm = _tile(r_, tm_pref, V7X_SUBLANES)
    rs = _row_spec(tm, d)
    return pl.pallas_call(
        _glu_kernel,
        out_shape=jax.ShapeDtypeStruct((s_, r_, d), F32),
        grid=(s_, r_ // tm),
        in_specs=[rs, _w2_spec(ng.shape), _mod_spec(sh, tm), _mod_spec(sc, tm),
                  _w2_spec(pw.shape), _w2_spec(pb.shape)],
        out_specs=rs,
        compiler_params=_cparams("arbitrary", "arbitrary"),
        name="glu",
    )(x, ng, sh, sc, pw, pb)


def _conv_tail(z, x, lng, lnb, pw2, pb2, gt, ng, sh, sc, rw, rb):
    mean = jnp.mean(z, axis=-1, keepdims=True)
    dlt = z - mean
    var = jnp.mean(dlt * dlt, axis=-1, keepdims=True)
    zn = dlt * lax.rsqrt(var + LN_EPS) * lng + lnb
    o = _bdot(jax.nn.silu(zn), pw2) + pb2
    x3 = x + gt * o
    h3 = _rms_mod(x3, ng, sh, sc)
    logits = _dot3(h3, rw) + rb
    n_exp = rb.shape[-1]
    lane = lax.broadcasted_iota(jnp.int32, logits.shape, 1)
    neg = jnp.float32(-jnp.inf)
    lg = logits
    v1 = jnp.max(lg, axis=-1, keepdims=True)
    i1 = jnp.min(jnp.where(lg == v1, lane, n_exp), axis=-1, keepdims=True)
    lg2 = jnp.where(lane == i1, neg, lg)
    v2 = jnp.max(lg2, axis=-1, keepdims=True)
    i2 = jnp.min(jnp.where(lg2 == v2, lane, n_exp), axis=-1, keepdims=True)
    e2 = jnp.exp(v2 - v1)
    den = 1.0 + e2
    comb = jnp.where(lane == i1, 1.0 / den, 0.0) + jnp.where(lane == i2, e2 / den, 0.0)
    return x3, h3, comb


def _conv_seq_kernel(width, u_ref, x_ref, dw_ref, db_ref, lng_ref, lnb_ref, pw2_ref, pb2_ref,
                     gt_ref, ng_ref, sh_ref, sc_ref, rw_ref, rb_ref,
                     x3_out, h3_out, comb_out, cst_out, ext, shifted):
    tm = u_ref.shape[0]
    halo = ext.shape[0] - tm
    t = pl.program_id(1)

    @pl.when(t == 0)
    def _():
        ext[0:halo, :] = jnp.zeros((halo, ext.shape[1]), F32)

    @pl.when(t > 0)
    def _():
        ext[0:halo, :] = ext[tm:tm + halo, :]
    ext[halo:halo + tm, :] = u_ref[...]
    off = halo - (width - 1)
    z = None
    for res in range(V7X_SUBLANES):
        taps = [wi for wi in range(width) if (off + wi) % V7X_SUBLANES == res]
        if not taps:
            continue
        span = (off + taps[-1]) - res
        if res == 0:
            src = ext
        else:
            src = shifted.at[res - 1]
            src[0:span + tm, :] = ext[res:res + span + tm, :]
        for wi in taps:
            lo = off + wi - res
            term = src[lo:lo + tm, :] * dw_ref[wi:wi + 1, :]
            z = term if z is None else z + term
    z = z + db_ref[...]
    x3, h3, comb = _conv_tail(z, x_ref[...], lng_ref[...], lnb_ref[...], pw2_ref[...], pb2_ref[...],
                              gt_ref[...], ng_ref[...], sh_ref[...], sc_ref[...], rw_ref[...],
                              rb_ref[...])
    x3_out[...] = x3
    h3_out[...] = h3.astype(BF16)
    comb_out[...] = comb
    cst_out[...] = ext[halo + tm - (width - 1):halo + tm, :]


def _conv_seq(u, x, cw, gt, ng, sh, sc, rw, rb, tm_pref=512):
    s_, r_, d = x.shape
    width = cw["dw"].shape[0]
    tm = _tile(r_, tm_pref, V7X_SUBLANES)
    halo = -(-(width - 1) // V7X_SUBLANES) * V7X_SUBLANES
    assert tm >= halo and r_ >= width - 1
    rs = _row_spec(tm, d)
    ws = [cw["dw"], cw["db"], cw["lng"], cw["lnb"], cw["pw2"], cw["pb2"]]
    n_lane = rw.shape[1]
    return pl.pallas_call(
        functools.partial(_conv_seq_kernel, width),
        out_shape=[jax.ShapeDtypeStruct((s_, r_, d), F32), jax.ShapeDtypeStruct((s_, r_, d), BF16),
                   jax.ShapeDtypeStruct((s_, r_, n_lane), F32),
                   jax.ShapeDtypeStruct((s_, width - 1, d), F32)],
        grid=(s_, r_ // tm),
        in_specs=[rs, rs] + [_w2_spec(w.shape) for w in ws]
        + [_mod_spec(gt, tm), _w2_spec(ng.shape), _mod_spec(sh, tm), _mod_spec(sc, tm),
           _w2_spec(rw.shape), _w2_spec(rb.shape)],
        out_specs=[rs, rs, pl.BlockSpec((None, tm, n_lane), lambda s, t: (s, t, 0)),
                   pl.BlockSpec((None, width - 1, d), lambda s, t: (s, 0, 0))],
        scratch_shapes=[pltpu.VMEM((tm + halo, d), F32),
                        pltpu.VMEM((V7X_SUBLANES - 1, tm + halo, d), F32)],
        compiler_params=_cparams("arbitrary", "arbitrary"),
        name="conv_seq",
    )(u, x, *ws, gt, ng, sh, sc, rw, rb)


def _conv_step_kernel(width, u_ref, x_ref, buf_ref, dw_ref, db_ref, lng_ref, lnb_ref, pw2_ref,
                      pb2_ref, gt_ref, ng_ref, sh_ref, sc_ref, rw_ref, rb_ref,
                      x3_out, h3_out, comb_out, cst_out):
    u = u_ref[...]
    z = u * dw_ref[width - 1:width, :] + db_ref[...]
    for wi in range(width - 1):
        z = z + buf_ref[:, wi, :] * dw_ref[wi:wi + 1, :]
    x3, h3, comb = _conv_tail(z, x_ref[...], lng_ref[...], lnb_ref[...], pw2_ref[...], pb2_ref[...],
                              gt_ref[...], ng_ref[...], sh_ref[...], sc_ref[...], rw_ref[...],
                              rb_ref[...])
    x3_out[...] = x3
    h3_out[...] = h3.astype(BF16)
    comb_out[...] = comb
    for wi in range(width - 2):
        cst_out[:, wi, :] = buf_ref[:, wi + 1, :]
    cst_out[:, width - 2, :] = u


def _conv_step(u, x, buf, cw, gt, ng, sh, sc, rw, rb, tm_pref=32):
    s_, r_, d = x.shape
    width = cw["dw"].shape[0]
    tm = _tile(r_, tm_pref, V7X_SUBLANES)
    rs = _row_spec(tm, d)
    bs = pl.BlockSpec((tm, width - 1, d), lambda s, t: (t, 0, 0))
    ws = [cw["dw"], cw["db"], cw["lng"], cw["lnb"], cw["pw2"], cw["pb2"]]
    n_lane = rw.shape[1]
    return pl.pallas_call(
        functools.partial(_conv_step_kernel, width),
        out_shape=[jax.ShapeDtypeStruct((s_, r_, d), F32), jax.ShapeDtypeStruct((s_, r_, d), BF16),
                   jax.ShapeDtypeStruct((s_, r_, n_lane), F32),
                   jax.ShapeDtypeStruct((r_, width - 1, d), F32)],
        grid=(s_, r_ // tm),
        in_specs=[rs, rs, bs] + [_w2_spec(w.shape) for w in ws]
        + [_mod_spec(gt, tm), _w2_spec(ng.shape), _mod_spec(sh, tm), _mod_spec(sc, tm),
           _w2_spec(rw.shape), _w2_spec(rb.shape)],
        out_specs=[rs, rs, pl.BlockSpec((None, tm, n_lane), lambda s, t: (s, t, 0)), bs],
        compiler_params=_cparams("arbitrary", "arbitrary"),
        name="conv_step",
    )(u, x, buf, *ws, gt, ng, sh, sc, rw, rb)


MOE_ROWS = 256
MOE_ROWS_TAIL = 128


def _moe_kernel(rows_full, rows_tail, h_ref, comb_ref, x_ref, w1_ref, w3_ref, w2_ref, gt_ref, fg_ref,
                fsh_ref, fsc_ref, y_out, acc, part, sel_scr, rank_scr, sel_t_scr, rank_t_scr):
    e = pl.program_id(2)
    tm = h_ref.shape[0]

    @pl.when(e == 0)
    def _():
        acc[...] = jnp.zeros_like(acc)
        sel = (comb_ref[...] > 0.0).astype(F32)
        row_i = lax.broadcasted_iota(jnp.int32, (tm, tm), 0)
        col_i = lax.broadcasted_iota(jnp.int32, (tm, tm), 1)
        before = (col_i < row_i).astype(BF16)
        rank = jnp.dot(before, sel.astype(BF16), preferred_element_type=F32)
        sel_scr[...] = sel
        rank_scr[...] = rank
        sel_t_scr[...] = sel.T
        rank_t_scr[...] = rank.T

    lane = lax.broadcasted_iota(jnp.int32, (tm, V7X_LANES), 1)

    def column(a):
        return jnp.sum(jnp.where(lane == e, a, 0.0), axis=-1, keepdims=True)

    sel_c, rank_c, gate_c = column(sel_scr[...]), column(rank_scr[...]), column(comb_ref[...])
    sel_r = sel_t_scr[pl.ds(e, 1), :]
    rank_r = rank_t_scr[pl.ds(e, 1), :]
    count = jnp.sum(sel_r).astype(jnp.int32)
    part[...] = jnp.zeros_like(part)

    def expert_pass(base, rows):
        basef = base.astype(F32)
        slot_r = lax.broadcasted_iota(jnp.int32, (rows, tm), 0).astype(F32) + basef
        gather = jnp.where((rank_r == slot_r) & (sel_r > 0.0), 1.0, 0.0).astype(BF16)
        xs = jnp.dot(gather, h_ref[...], preferred_element_type=F32).astype(BF16)
        a = jax.nn.silu(jnp.dot(xs, w1_ref[...], preferred_element_type=F32))
        b = jnp.dot(xs, w3_ref[...], preferred_element_type=F32)
        ys = _bdot(a * b, w2_ref[...])
        slot_c = lax.broadcasted_iota(jnp.int32, (tm, rows), 1).astype(F32) + basef
        scatter = jnp.where((rank_c == slot_c) & (sel_c > 0.0), 1.0, 0.0).astype(BF16)
        part[...] += jnp.dot(scatter, ys.astype(BF16), preferred_element_type=F32)

    n_full = count // rows_full

    def full_pass(i, carry):
        expert_pass(i * rows_full, rows_full)
        return carry
    lax.fori_loop(0, n_full, full_pass, 0)
    rem = count - n_full * rows_full

    @pl.when(rem > rows_tail)
    def _():
        expert_pass(n_full * rows_full, rows_full)

    @pl.when((rem > 0) & (rem <= rows_tail))
    def _():
        expert_pass(n_full * rows_full, rows_tail)
    acc[...] += gate_c * part[...]

    @pl.when(e == pl.num_programs(2) - 1)
    def _():
        x4 = x_ref[...] + gt_ref[...] * acc[...]
        y_out[...] = _rms_mod(x4, fg_ref[...], fsh_ref[...], fsc_ref[...])


def _moe(h3, comb, x3, w1, w3, w2, gt, fg, fsh, fsc, tm_pref=1024):
    s_, r_, d = x3.shape
    n_exp, _, fe = w1.shape
    tm = _tile(r_, tm_pref, V7X_LANES)
    n_lane = comb.shape[-1]
    assert n_lane == V7X_LANES
    body = functools.partial(_moe_kernel, min(MOE_ROWS, tm), min(MOE_ROWS_TAIL, tm))
    rs = pl.BlockSpec((None, tm, d), lambda s, t, e: (s, t, 0))

    def ms(arr):
        if arr.shape[1] == 1:
            return pl.BlockSpec((None, 1, d), lambda s, t, e: (s, 0, 0))
        return rs
    return pl.pallas_call(
        body,
        out_shape=jax.ShapeDtypeStruct((s_, r_, d), F32),
        grid=(s_, r_ // tm, n_exp),
        in_specs=[rs, pl.BlockSpec((None, tm, n_lane), lambda s, t, e: (s, t, 0)), rs,
                  pl.BlockSpec((None, d, fe), lambda s, t, e: (e, 0, 0)),
                  pl.BlockSpec((None, d, fe), lambda s, t, e: (e, 0, 0)),
                  pl.BlockSpec((None, fe, d), lambda s, t, e: (e, 0, 0)),
                  ms(gt), pl.BlockSpec(fg.shape, lambda s, t, e: (0, 0)), ms(fsh), ms(fsc)],
        out_specs=rs,
        scratch_shapes=[pltpu.VMEM((tm, d), F32), pltpu.VMEM((tm, d), F32),
                        pltpu.VMEM((tm, n_lane), F32), pltpu.VMEM((tm, n_lane), F32),
                        pltpu.VMEM((n_lane, tm), F32), pltpu.VMEM((n_lane, tm), F32)],
        compiler_params=_cparams("arbitrary", "arbitrary", "arbitrary"),
        name="moe",
    )(h3, comb, x3, w1, w3, w2, gt, fg, fsh, fsc)


def _trunk(x, mods0, mods1, modf, wkv0, shift0, conv0, P):
    s_, r_, d = x.shape
    heads = d // HEAD_SIZE
    sh1, sc1, gt1, sh2, sc2, gt2 = mods0
    r, k, v, w, a, g, h_last = _pre0(x, P["nmix0"], sh1, sc1, shift0, P["rw"])
    if wkv0 is None:
        ops = [_to_lanes(t, heads) for t in (r, w, k, v, a)]
        z, s_fin = _scan(*ops, None, P["scan_consts"])
        z = _from_lanes(z, s_, heads)
        n_seq = s_
    else:
        ops = [_to_lanes(t.reshape(r_, 1, d), heads) for t in (r, w, k, v, a)]
        z, s_fin = _scan(*ops, wkv0, P["scan_consts"])
        z = _from_lanes(z, r_, heads).reshape(1, r_, d)
        n_seq = r_
    wkv_out = jnp.transpose(s_fin.reshape(HEAD_SIZE, HEAD_SIZE, n_seq, heads), (2, 3, 1, 0))
    x1, h2 = _post0(z, g, x, P["wo"], gt1, P["nffn0"], sh2, sc2)
    x2 = _ffn(h2, x1, P["ffn_w1"], P["ffn_w3"], P["ffn_w2"], gt2)
    sh1, sc1, gt1, sh2, sc2, gt2 = mods1
    u = _glu(x2, P["nmix1"], sh1, sc1, P["pw1"], P["pb1"])
    if conv0 is None:
        x3, h3, comb, conv_out = _conv_seq(u, x2, P["cv"], gt1, P["nffn1"], sh2, sc2,
                                           P["router_w"], P["router_b"])
    else:
        x3, h3, comb, conv_out = _conv_step(u, x2, conv0, P["cv"], gt1, P["nffn1"], sh2, sc2,
                                            P["router_w"], P["router_b"])
    fsh, fsc = modf
    y = _moe(h3, comb, x3, P["moe_w1"], P["moe_w3"], P["moe_w2"], gt2, P["final_g"], fsh, fsc)
    return y, wkv_out, h_last, conv_out


def kernel(x_prompt, x_sample, state_wkv, state_shift, state_conv, c_prompt, c_sample, ada_w, ada_b, norm_mix_g, norm_ffn_g, rw_mu, rw_wr, rw_wk, rw_wv, rw_wo, rw_w0, rw_w1, rw_w2, rw_a0, rw_a1, rw_a2, rw_g1, rw_g2, rw_kk, rw_ka, rw_rk, rw_lnx_g, rw_lnx_b, cv_pw1_w, cv_pw1_b, cv_dw_w, cv_dw_b, cv_ln_g, cv_ln_b, cv_pw2_w, cv_pw2_b, ffn_w1, ffn_w3, ffn_w2, moe_router_w, moe_router_b, moe_w1, moe_w3, moe_w2, final_g, final_ada_w, final_ada_b):
    nb, seq, d = x_prompt.shape
    db = x_sample.shape[0]
    heads = d // HEAD_SIZE
    n_exp = moe_router_w.shape[-1]
    assert (nb * heads) % V7X_LANES == 0 and (db * heads) % V7X_LANES == 0
    assert V7X_LANES % heads == 0 and x_sample.shape[1] == 1

    def row(vec):
        return vec.reshape(1, -1)

    def lane_tile(vec):
        t = vec.reshape(heads, HEAD_SIZE).T
        return jnp.tile(t, (1, V7X_LANES // heads))

    bf = lambda t: t.astype(BF16)
    P = {
        "nmix0": row(norm_mix_g[0]), "nffn0": row(norm_ffn_g[0]),
        "nmix1": row(norm_mix_g[1]), "nffn1": row(norm_ffn_g[1]),
        "rw": {"mu": rw_mu[0], "wr": bf(rw_wr[0]), "wk": bf(rw_wk[0]), "wv": bf(rw_wv[0]),
               "w1": bf(rw_w1[0]), "w2": bf(rw_w2[0]), "a1": bf(rw_a1[0]), "a2": bf(rw_a2[0]),
               "g1": bf(rw_g1[0]), "g2": bf(rw_g2[0]), "w0": row(rw_w0[0]), "a0": row(rw_a0[0])},
        "scan_consts": [lane_tile(rw_kk[0]), lane_tile(rw_ka[0]), lane_tile(rw_rk[0].reshape(-1)),
                        lane_tile(rw_lnx_g[0]), lane_tile(rw_lnx_b[0])],
        "wo": bf(rw_wo[0]),
        "ffn_w1": bf(ffn_w1[0]), "ffn_w3": bf(ffn_w3[0]), "ffn_w2": bf(ffn_w2[0]),
        "pw1": bf(cv_pw1_w[0]), "pb1": row(cv_pw1_b[0]),
        "cv": {"dw": cv_dw_w[0], "db": row(cv_dw_b[0]), "lng": row(cv_ln_g[0]),
               "lnb": row(cv_ln_b[0]), "pw2": bf(cv_pw2_w[0]), "pb2": row(cv_pw2_b[0])},
        "router_w": jnp.pad(moe_router_w[0], ((0, 0), (0, V7X_LANES - n_exp))),
        "router_b": jnp.pad(row(moe_router_b[0]), ((0, 0), (0, V7X_LANES - n_exp)),
                            constant_values=-jnp.inf),
        "moe_w1": bf(moe_w1[0]), "moe_w3": bf(moe_w3[0]), "moe_w2": bf(moe_w2[0]),
        "final_g": row(final_g),
    }

    c_all = jnp.concatenate([c_prompt, c_sample], axis=0)
    ada = [_ada(c_all, ada_w[i], ada_b[i]) for i in range(ada_w.shape[0])]
    ada_f = _ada(c_all, final_ada_w, final_ada_b)

    def mods(p, n, prompt):
        parts = jnp.split(p, n, axis=-1)
        if prompt:
            return [t[:nb].reshape(nb, 1, d) for t in parts]
        return [t[nb:].reshape(1, db, d) for t in parts]

    y_p, wkv_p, shift_p, conv_p = _trunk(
        x_prompt, mods(ada[0], 6, True), mods(ada[1], 6, True), mods(ada_f, 2, True),
        None, None, None, P)
    wkv0 = jnp.transpose(state_wkv[0], (3, 2, 0, 1)).reshape(HEAD_SIZE, HEAD_SIZE, db * heads)
    y_s, wkv_s, shift_s, conv_s = _trunk(
        x_sample.reshape(1, db, d), mods(ada[0], 6, False), mods(ada[1], 6, False),
        mods(ada_f, 2, False), wkv0, state_shift[0].reshape(1, db, d), state_conv[0], P)
    return (y_p, y_s.reshape(db, 1, d), wkv_p[None], shift_p.reshape(1, nb, d), conv_p[None],
            wkv_s[None], shift_s.reshape(1, db, d), conv_s[None])
```

```python
import functools

import jax
import jax.numpy as jnp
from jax import lax
from jax.experimental import pallas as pl
from jax.experimental.pallas import tpu as pltpu

F32 = jnp.float32
BF16 = jnp.bfloat16

HEAD_SIZE = 64
RMS_EPS = 1e-6
LN_EPS = 1e-5
GN_EPS = 64e-5
MIN_DECAY = 0.5452
SCAN_CHUNK = 64

V7X_LANES = 128
V7X_SUBLANES = 8
V7X_VMEM_BYTES = 64 * 1024 * 1024
VMEM_LIMIT_BYTES = V7X_VMEM_BYTES - 8 * 1024 * 1024


def _cparams(*sem):
    return pltpu.CompilerParams(dimension_semantics=sem, vmem_limit_bytes=VMEM_LIMIT_BYTES)


def _tile(n, pref, mult):
    if n <= pref:
        return n
    t = (pref // mult) * mult
    while t >= mult:
        if n % t == 0:
            return t
        t -= mult
    return n


def _bdot(a, b):
    return jnp.dot(a.astype(BF16), b.astype(BF16), preferred_element_type=F32)


def _split_bf16(x):
    hi = x.astype(BF16)
    lo = (x - hi.astype(F32)).astype(BF16)
    return hi, lo


def _dot3(a, b):
    a_hi, a_lo = _split_bf16(a)
    b_hi, b_lo = _split_bf16(b)
    d = functools.partial(jnp.dot, preferred_element_type=F32)
    return d(a_hi, b_hi) + d(a_hi, b_lo) + d(a_lo, b_hi)


def _rms_mod(x, g, shift, scale):
    ms = jnp.mean(x * x, axis=-1, keepdims=True)
    y = x * lax.rsqrt(ms + RMS_EPS) * g
    return y * (1.0 + scale) + shift


def _const_spec(shape):
    nd = len(shape)
    return pl.BlockSpec(shape, lambda *_: (0,) * nd)


def _ada_kernel(c_ref, w_ref, b_ref, o_ref):
    s = jax.nn.silu(c_ref[...])
    o_ref[...] = _dot3(s, w_ref[...]) + b_ref[...]


def _ada(c, w, b):
    rows, d = c.shape
    n = w.shape[1]
    tn = _tile(n, 1024, V7X_LANES)
    return pl.pallas_call(
        _ada_kernel,
        out_shape=jax.ShapeDtypeStruct((rows, n), F32),
        grid=(n // tn,),
        in_specs=[
            pl.BlockSpec((rows, d), lambda j: (0, 0)),
            pl.BlockSpec((d, tn), lambda j: (0, j)),
            pl.BlockSpec((1, tn), lambda j: (0, j)),
        ],
        out_specs=pl.BlockSpec((rows, tn), lambda j: (0, j)),
        compiler_params=_cparams("arbitrary"),
        name="ada",
    )(c, w, b.reshape(1, n))


def _row_spec(tm, d):
    return pl.BlockSpec((None, tm, d), lambda s, t: (s, t, 0))


def _mod_spec(arr, tm):
    d = arr.shape[-1]
    if arr.shape[1] == 1:
        return pl.BlockSpec((None, 1, d), lambda s, t: (s, 0, 0))
    return pl.BlockSpec((None, tm, d), lambda s, t: (s, t, 0))


def _w2_spec(shape):
    return pl.BlockSpec(shape, lambda s, t: (0, 0))


def _pre0_kernel(sequential, *refs):
    if sequential:
        (x_ref, g_ref, sh_ref, sc_ref, mu_ref, wr_ref, wk_ref, wv_ref, w1_ref, w2_ref, a1_ref,
         a2_ref, g1_ref, g2_ref, w0_ref, a0_ref,
         r_out, k_out, v_out, w_out, a_out, g_out, h_out, carry) = refs
        hprev_ref = None
    else:
        (x_ref, g_ref, sh_ref, sc_ref, hprev_ref, mu_ref, wr_ref, wk_ref, wv_ref, w1_ref, w2_ref,
         a1_ref, a2_ref, g1_ref, g2_ref, w0_ref, a0_ref,
         r_out, k_out, v_out, w_out, a_out, g_out, h_out) = refs
    h = _rms_mod(x_ref[...], g_ref[...], sh_ref[...], sc_ref[...])
    tm = h.shape[0]
    if sequential:
        @pl.when(pl.program_id(1) == 0)
        def _():
            carry[...] = jnp.zeros_like(carry)
        row = lax.broadcasted_iota(jnp.int32, h.shape, 0)
        h_prev = jnp.where(row == 0, carry[...], pltpu.roll(h, 1, 0))
        carry[...] = h[tm - 1:tm, :]
        h_out[...] = h[tm - 1:tm, :]
    else:
        h_prev = hprev_ref[...]
        h_out[...] = h
    xx = h_prev - h
    mu = mu_ref[...]
    xr, xw, xk, xv, xa, xg = [h + xx * mu[i:i + 1, :] for i in range(6)]
    r_out[...] = _bdot(xr, wr_ref[...])
    k_out[...] = _bdot(xk, wk_ref[...])
    v_out[...] = _bdot(xv, wv_ref[...])
    w_raw = w0_ref[...] + _bdot(jnp.tanh(_bdot(xw, w1_ref[...])), w2_ref[...])
    w_out[...] = jnp.exp(-jnp.exp(-0.5) * jax.nn.sigmoid(w_raw))
    a_out[...] = jax.nn.sigmoid(a0_ref[...] + _bdot(_bdot(xa, a1_ref[...]), a2_ref[...]))
    g_out[...] = _bdot(jax.nn.sigmoid(_bdot(xg, g1_ref[...])), g2_ref[...])


def _pre0(x, g, sh, sc, h_prev, p, tm_pref=256):
    s_, r_, d = x.shape
    tm = _tile(r_, tm_pref, V7X_SUBLANES)
    sequential = h_prev is None
    weights = [p["mu"], p["wr"], p["wk"], p["wv"], p["w1"], p["w2"], p["a1"], p["a2"], p["g1"],
               p["g2"], p["w0"], p["a0"]]
    ins = [x, g, sh, sc] + ([] if sequential else [h_prev]) + weights
    in_specs = [_row_spec(tm, d), _w2_spec(g.shape), _mod_spec(sh, tm), _mod_spec(sc, tm)]
    if not sequential:
        in_specs.append(_row_spec(tm, d))
    in_specs += [_w2_spec(w.shape) for w in weights]
    big = jax.ShapeDtypeStruct((s_, r_, d), F32)
    if sequential:
        h_shape = jax.ShapeDtypeStruct((s_, 1, d), F32)
        h_spec = pl.BlockSpec((None, 1, d), lambda s, t: (s, 0, 0))
        scratch = [pltpu.VMEM((1, d), F32)]
    else:
        h_shape, h_spec, scratch = big, _row_spec(tm, d), []
    return pl.pallas_call(
        functools.partial(_pre0_kernel, sequential),
        out_shape=[big] * 6 + [h_shape],
        grid=(s_, r_ // tm),
        in_specs=in_specs,
        out_specs=[_row_spec(tm, d)] * 6 + [h_spec],
        scratch_shapes=scratch,
        compiler_params=_cparams("arbitrary", "arbitrary"),
        name="pre0",
    )(*ins)


def _scan_kernel(zero_init, tc_len, *refs):
    if zero_init:
        (r_ref, w_ref, k_ref, v_ref, a_ref, kk_ref, ka_ref, rk_ref, lg_ref, lb_ref,
         z_ref, sT_ref, s_scr, ab_scr) = refs
        s0_ref = None
    else:
        (r_ref, w_ref, k_ref, v_ref, a_ref, s0_ref, kk_ref, ka_ref, rk_ref, lg_ref, lb_ref,
         z_ref, sT_ref, s_scr, ab_scr) = refs
    n = HEAD_SIZE

    @pl.when(pl.program_id(1) == 0)
    def _():
        if zero_init:
            s_scr[...] = jnp.zeros_like(s_scr)
        else:
            s_scr[...] = s0_ref[...]

    kk_t, ka_t, rk_t, lg_t, lb_t = kk_ref[...], ka_ref[...], rk_ref[...], lg_ref[...], lb_ref[...]

    def step(t, c_prev):
        k = k_ref[t]
        a = a_ref[t]
        r = r_ref[t]
        kk = k * kk_t
        nrm = jnp.sqrt(jnp.sum(kk * kk, axis=0, keepdims=True))
        kk = kk / jnp.maximum(nrm, 1e-12)
        k_eff = k * (1.0 + (a - 1.0) * ka_t)
        bonus_dot = jnp.sum(r * k_eff * rk_t, axis=0, keepdims=True)
        c_new = c_prev * w_ref[t]
        c_inv = 1.0 / c_new
        ab_scr[0] = -kk * c_prev
        ab_scr[1] = kk * a * c_inv
        ab_scr[2] = k_eff * c_inv
        ab_scr[3] = r * c_new
        v = v_ref[t]
        sa = s_scr[0] * ab_scr[0, 0:1, :]
        for j in range(1, n):
            sa = sa + s_scr[j] * ab_scr[0, j:j + 1, :]
        y = None
        for j in range(n):
            q_new = s_scr[j] + sa * ab_scr[1, j:j + 1, :] + v * ab_scr[2, j:j + 1, :]
            s_scr[j] = q_new
            term = q_new * ab_scr[3, j:j + 1, :]
            y = term if y is None else y + term
        mean = jnp.mean(y, axis=0, keepdims=True)
        dlt = y - mean
        var = jnp.mean(dlt * dlt, axis=0, keepdims=True)
        yn = dlt * lax.rsqrt(var + GN_EPS) * lg_t + lb_t
        z_ref[t] = yn + bonus_dot * v_ref[t]
        return c_new

    c_fin = lax.fori_loop(0, tc_len, step, jnp.ones((n, V7X_LANES), F32))
    ab_scr[0] = c_fin
    for j in range(n):
        s_scr[j] = s_scr[j] * ab_scr[0, j:j + 1, :]

    @pl.when(pl.program_id(1) == pl.num_programs(1) - 1)
    def _():
        sT_ref[...] = s_scr[...]


def _scan(r, w, k, v, a, s0, consts):
    t_len, n, inst = r.shape
    tc = _tile(t_len, SCAN_CHUNK, 1)
    assert MIN_DECAY ** tc > 1e-30
    groups = inst // V7X_LANES
    op_spec = pl.BlockSpec((tc, n, V7X_LANES), lambda g, c: (c, 0, g))
    st_spec = pl.BlockSpec((n, n, V7X_LANES), lambda g, c: (0, 0, g))
    c_spec = pl.BlockSpec((n, V7X_LANES), lambda g, c: (0, 0))
    zero_init = s0 is None
    ins = [r, w, k, v, a] + ([] if zero_init else [s0]) + list(consts)
    in_specs = [op_spec] * 5 + ([] if zero_init else [st_spec]) + [c_spec] * len(consts)
    return pl.pallas_call(
        functools.partial(_scan_kernel, zero_init, tc),
        out_shape=[jax.ShapeDtypeStruct((t_len, n, inst), F32),
                   jax.ShapeDtypeStruct((n, n, inst), F32)],
        grid=(groups, t_len // tc),
        in_specs=in_specs,
        out_specs=[op_spec, st_spec],
        scratch_shapes=[pltpu.VMEM((n, n, V7X_LANES), F32), pltpu.VMEM((4, n, V7X_LANES), F32)],
        compiler_params=_cparams("arbitrary", "arbitrary"),
        name="wkv_scan",
    )(*ins)


def _to_lanes(x, heads):
    s_, r_, d = x.shape
    return jnp.transpose(x.reshape(s_, r_, heads, HEAD_SIZE), (1, 3, 0, 2)).reshape(
        r_, HEAD_SIZE, s_ * heads)


def _from_lanes(z, s_, heads):
    t_len = z.shape[0]
    return jnp.transpose(z.reshape(t_len, HEAD_SIZE, s_, heads), (2, 0, 3, 1)).reshape(
        s_, t_len, heads * HEAD_SIZE)


def _post0_kernel(z_ref, g_ref, x_ref, wo_ref, gt_ref, ng_ref, sh_ref, sc_ref, x1_out, h2_out):
    o = _bdot(z_ref[...] * g_ref[...], wo_ref[...])
    x1 = x_ref[...] + gt_ref[...] * o
    x1_out[...] = x1
    h2_out[...] = _rms_mod(x1, ng_ref[...], sh_ref[...], sc_ref[...]).astype(BF16)


def _post0(z, g, x, wo, gt, ng, sh, sc, tm_pref=512):
    s_, r_, d = x.shape
    tm = _tile(r_, tm_pref, V7X_SUBLANES)
    rs = _row_spec(tm, d)
    return pl.pallas_call(
        _post0_kernel,
        out_shape=[jax.ShapeDtypeStruct((s_, r_, d), F32), jax.ShapeDtypeStruct((s_, r_, d), BF16)],
        grid=(s_, r_ // tm),
        in_specs=[rs, rs, rs, _w2_spec(wo.shape), _mod_spec(gt, tm), _w2_spec(ng.shape),
                  _mod_spec(sh, tm), _mod_spec(sc, tm)],
        out_specs=[rs, rs],
        compiler_params=_cparams("arbitrary", "arbitrary"),
        name="post0",
    )(z, g, x, wo, gt, ng, sh, sc)


def _ffn_kernel(h_ref, x_ref, w1_ref, w3_ref, w2_ref, gt_ref, x2_out, acc):
    f = pl.program_id(2)

    @pl.when(f == 0)
    def _():
        acc[...] = jnp.zeros_like(acc)
    h = h_ref[...]
    a = jax.nn.silu(jnp.dot(h, w1_ref[...], preferred_element_type=F32))
    b = jnp.dot(h, w3_ref[...], preferred_element_type=F32)
    acc[...] += _bdot(a * b, w2_ref[...])

    @pl.when(f == pl.num_programs(2) - 1)
    def _():
        x2_out[...] = x_ref[...] + gt_ref[...] * acc[...]


def _ffn(h2, x1, w1, w3, w2, gt, tm_pref=512, tf_pref=1408):
    s_, r_, d = x1.shape
    ff = w1.shape[1]
    tm = _tile(r_, tm_pref, V7X_SUBLANES)
    tf = _tile(ff, tf_pref, V7X_LANES)
    rs = pl.BlockSpec((None, tm, d), lambda s, t, f: (s, t, 0))
    if gt.shape[1] == 1:
        gs = pl.BlockSpec((None, 1, d), lambda s, t, f: (s, 0, 0))
    else:
        gs = rs
    return pl.pallas_call(
        _ffn_kernel,
        out_shape=jax.ShapeDtypeStruct((s_, r_, d), F32),
        grid=(s_, r_ // tm, ff // tf),
        in_specs=[rs, rs,
                  pl.BlockSpec((d, tf), lambda s, t, f: (0, f)),
                  pl.BlockSpec((d, tf), lambda s, t, f: (0, f)),
                  pl.BlockSpec((tf, d), lambda s, t, f: (f, 0)),
                  gs],
        out_specs=rs,
        scratch_shapes=[pltpu.VMEM((tm, d), F32)],
        compiler_params=_cparams("arbitrary", "arbitrary", "arbitrary"),
        name="ffn",
    )(h2, x1, w1, w3, w2, gt)


def _glu_kernel(x_ref, ng_ref, sh_ref, sc_ref, pw_ref, pb_ref, u_out):
    d = x_ref.shape[-1]
    h = _rms_mod(x_ref[...], ng_ref[...], sh_ref[...], sc_ref[...])
    u = _bdot(h, pw_ref[...]) + pb_ref[...]
    u_out[...] = u[:, :d] * jax.nn.sigmoid(u[:, d:])


def _glu(x, ng, sh, sc, pw, pb, tm_pref=512):
    s_, r_, d = x.shape
    tm = _tile(r_, tm_pref, V7X_SUBLANES)
    rs = _row_spec(tm, d)
    return pl.pallas_call(
        _glu_kernel,
        out_shape=jax.ShapeDtypeStruct((s_, r_, d), F32),
        grid=(s_, r_ // tm),
        in_specs=[rs, _w2_spec(ng.shape), _mod_spec(sh, tm), _mod_spec(sc, tm),
                  _w2_spec(pw.shape), _w2_spec(pb.shape)],
        out_specs=rs,
        compiler_params=_cparams("arbitrary", "arbitrary"),
        name="glu",
    )(x, ng, sh, sc, pw, pb)


def _conv_tail(z, x, lng, lnb, pw2, pb2, gt, ng, sh, sc, rw, rb):
    mean = jnp.mean(z, axis=-1, keepdims=True)
    dlt = z - mean
    var = jnp.mean(dlt * dlt, axis=-1, keepdims=True)
    zn = dlt * lax.rsqrt(var + LN_EPS) * lng + lnb
    o = _bdot(jax.nn.silu(zn), pw2) + pb2
    x3 = x + gt * o
    h3 = _rms_mod(x3, ng, sh, sc)
    logits = _dot3(h3, rw) + rb
    n_exp = rb.shape[-1]
    lane = lax.broadcasted_iota(jnp.int32, logits.shape, 1)
    neg = jnp.float32(-jnp.inf)
    lg = logits
    v1 = jnp.max(lg, axis=-1, keepdims=True)
    i1 = jnp.min(jnp.where(lg == v1, lane, n_exp), axis=-1, keepdims=True)
    lg2 = jnp.where(lane == i1, neg, lg)
    v2 = jnp.max(lg2, axis=-1, keepdims=True)
    i2 = jnp.min(jnp.where(lg2 == v2, lane, n_exp), axis=-1, keepdims=True)
    e2 = jnp.exp(v2 - v1)
    den = 1.0 + e2
    comb = jnp.where(lane == i1, 1.0 / den, 0.0) + jnp.where(lane == i2, e2 / den, 0.0)
    return x3, h3, comb


def _conv_seq_kernel(width, u_ref, x_ref, dw_ref, db_ref, lng_ref, lnb_ref, pw2_ref, pb2_ref,
                     gt_ref, ng_ref, sh_ref, sc_ref, rw_ref, rb_ref,
                     x3_out, h3_out, comb_out, cst_out, ext, shifted):
    tm = u_ref.shape[0]
    halo = ext.shape[0] - tm
    t = pl.program_id(1)

    @pl.when(t == 0)
    def _():
        ext[0:halo, :] = jnp.zeros((halo, ext.shape[1]), F32)

    @pl.when(t > 0)
    def _():
        ext[0:halo, :] = ext[tm:tm + halo, :]
    ext[halo:halo + tm, :] = u_ref[...]
    off = halo - (width - 1)
    z = None
    for res in range(V7X_SUBLANES):
        taps = [wi for wi in range(width) if (off + wi) % V7X_SUBLANES == res]
        if not taps:
            continue
        span = (off + taps[-1]) - res
        if res == 0:
            src = ext
        else:
            src = shifted.at[res - 1]
            src[0:span + tm, :] = ext[res:res + span + tm, :]
        for wi in taps:
            lo = off + wi - res
            term = src[lo:lo + tm, :] * dw_ref[wi:wi + 1, :]
            z = term if z is None else z + term
    z = z + db_ref[...]
    x3, h3, comb = _conv_tail(z, x_ref[...], lng_ref[...], lnb_ref[...], pw2_ref[...], pb2_ref[...],
                              gt_ref[...], ng_ref[...], sh_ref[...], sc_ref[...], rw_ref[...],
                              rb_ref[...])
    x3_out[...] = x3
    h3_out[...] = h3.astype(BF16)
    comb_out[...] = comb
    cst_out[...] = ext[halo + tm - (width - 1):halo + tm, :]


def _conv_seq(u, x, cw, gt, ng, sh, sc, rw, rb, tm_pref=512):
    s_, r_, d = x.shape
    width = cw["dw"].shape[0]
    tm = _tile(r_, tm_pref, V7X_SUBLANES)
    halo = -(-(width - 1) // V7X_SUBLANES) * V7X_SUBLANES
    assert tm >= halo and r_ >= width - 1
    rs = _row_spec(tm, d)
    ws = [cw["dw"], cw["db"], cw["lng"], cw["lnb"], cw["pw2"], cw["pb2"]]
    n_lane = rw.shape[1]
    return pl.pallas_call(
        functools.partial(_conv_seq_kernel, width),
        out_shape=[jax.ShapeDtypeStruct((s_, r_, d), F32), jax.ShapeDtypeStruct((s_, r_, d), BF16),
                   jax.ShapeDtypeStruct((s_, r_, n_lane), F32),
                   jax.ShapeDtypeStruct((s_, width - 1, d), F32)],
        grid=(s_, r_ // tm),
        in_specs=[rs, rs] + [_w2_spec(w.shape) for w in ws]
        + [_mod_spec(gt, tm), _w2_spec(ng.shape), _mod_spec(sh, tm), _mod_spec(sc, tm),
           _w2_spec(rw.shape), _w2_spec(rb.shape)],
        out_specs=[rs, rs, pl.BlockSpec((None, tm, n_lane), lambda s, t: (s, t, 0)),
                   pl.BlockSpec((None, width - 1, d), lambda s, t: (s, 0, 0))],
        scratch_shapes=[pltpu.VMEM((tm + halo, d), F32),
                        pltpu.VMEM((V7X_SUBLANES - 1, tm + halo, d), F32)],
        compiler_params=_cparams("arbitrary", "arbitrary"),
        name="conv_seq",
    )(u, x, *ws, gt, ng, sh, sc, rw, rb)


def _conv_step_kernel(width, u_ref, x_ref, buf_ref, dw_ref, db_ref, lng_ref, lnb_ref, pw2_ref,
                      pb2_ref, gt_ref, ng_ref, sh_ref, sc_ref, rw_ref, rb_ref,
                      x3_out, h3_out, comb_out, cst_out):
    u = u_ref[...]
    z = u * dw_ref[width - 1:width, :] + db_ref[...]
    for wi in range(width - 1):
        z = z + buf_ref[:, wi, :] * dw_ref[wi:wi + 1, :]
    x3, h3, comb = _conv_tail(z, x_ref[...], lng_ref[...], lnb_ref[...], pw2_ref[...], pb2_ref[...],
                              gt_ref[...], ng_ref[...], sh_ref[...], sc_ref[...], rw_ref[...],
                              rb_ref[...])
    x3_out[...] = x3
    h3_out[...] = h3.astype(BF16)
    comb_out[...] = comb
    for wi in range(width - 2):
        cst_out[:, wi, :] = buf_ref[:, wi + 1, :]
    cst_out[:, width - 2, :] = u


def _conv_step(u, x, buf, cw, gt, ng, sh, sc, rw, rb, tm_pref=32):
    s_, r_, d = x.shape
    width = cw["dw"].shape[0]
    tm = _tile(r_, tm_pref, V7X_SUBLANES)
    rs = _row_spec(tm, d)
    bs = pl.BlockSpec((tm, width - 1, d), lambda s, t: (t, 0, 0))
    ws = [cw["dw"], cw["db"], cw["lng"], cw["lnb"], cw["pw2"], cw["pb2"]]
    n_lane = rw.shape[1]
    return pl.pallas_call(
        functools.partial(_conv_step_kernel, width),
        out_shape=[jax.ShapeDtypeStruct((s_, r_, d), F32), jax.ShapeDtypeStruct((s_, r_, d), BF16),
                   jax.ShapeDtypeStruct((s_, r_, n_lane), F32),
                   jax.ShapeDtypeStruct((r_, width - 1, d), F32)],
        grid=(s_, r_ // tm),
        in_specs=[rs, rs, bs] + [_w2_spec(w.shape) for w in ws]
        + [_mod_spec(gt, tm), _w2_spec(ng.shape), _mod_spec(sh, tm), _mod_spec(sc, tm),
           _w2_spec(rw.shape), _w2_spec(rb.shape)],
        out_specs=[rs, rs, pl.BlockSpec((None, tm, n_lane), lambda s, t: (s, t, 0)), bs],
        compiler_params=_cparams("arbitrary", "arbitrary"),
        name="conv_step",
    )(u, x, buf, *ws, gt, ng, sh, sc, rw, rb)


MOE_ROWS = 256
MOE_ROWS_TAIL = 128


def _moe_kernel(rows_full, rows_tail, h_ref, comb_ref, x_ref, w1_ref, w3_ref, w2_ref, gt_ref, fg_ref,
                fsh_ref, fsc_ref, y_out, acc, part, sel_scr, rank_scr, sel_t_scr, rank_t_scr):
    e = pl.program_id(2)
    tm = h_ref.shape[0]

    @pl.when(e == 0)
    def _():
        acc[...] = jnp.zeros_like(acc)
        sel = (comb_ref[...] > 0.0).astype(F32)
        row_i = lax.broadcasted_iota(jnp.int32, (tm, tm), 0)
        col_i = lax.broadcasted_iota(jnp.int32, (tm, tm), 1)
        before = (col_i < row_i).astype(BF16)
        rank = jnp.dot(before, sel.astype(BF16), preferred_element_type=F32)
        sel_scr[...] = sel
        rank_scr[...] = rank
        sel_t_scr[...] = sel.T
        rank_t_scr[...] = rank.T

    lane = lax.broadcasted_iota(jnp.int32, (tm, V7X_LANES), 1)

    def column(a):
        return jnp.sum(jnp.where(lane == e, a, 0.0), axis=-1, keepdims=True)

    sel_c, rank_c, gate_c = column(sel_scr[...]), column(rank_scr[...]), column(comb_ref[...])
    sel_r = sel_t_scr[pl.ds(e, 1), :]
    rank_r = rank_t_scr[pl.ds(e, 1), :]
    count = jnp.sum(sel_r).astype(jnp.int32)
    part[...] = jnp.zeros_like(part)

    def expert_pass(base, rows):
        basef = base.astype(F32)
        slot_r = lax.broadcasted_iota(jnp.int32, (rows, tm), 0).astype(F32) + basef
        gather = jnp.where((rank_r == slot_r) & (sel_r > 0.0), 1.0, 0.0).astype(BF16)
        xs = jnp.dot(gather, h_ref[...], preferred_element_type=F32).astype(BF16)
        a = jax.nn.silu(jnp.dot(xs, w1_ref[...], preferred_element_type=F32))
        b = jnp.dot(xs, w3_ref[...], preferred_element_type=F32)
        ys = _bdot(a * b, w2_ref[...])
        slot_c = lax.broadcasted_iota(jnp.int32, (tm, rows), 1).astype(F32) + basef
        scatter = jnp.where((rank_c == slot_c) & (sel_c > 0.0), 1.0, 0.0).astype(BF16)
        part[...] += jnp.dot(scatter, ys.astype(BF16), preferred_element_type=F32)

    n_full = count // rows_full

    def full_pass(i, carry):
        expert_pass(i * rows_full, rows_full)
        return carry
    lax.fori_loop(0, n_full, full_pass, 0)
    rem = count - n_full * rows_full

    @pl.when(rem > rows_tail)
    def _():
        expert_pass(n_full * rows_full, rows_full)

    @pl.when((rem > 0) & (rem <= rows_tail))
    def _():
        expert_pass(n_full * rows_full, rows_tail)
    acc[...] += gate_c * part[...]

    @pl.when(e == pl.num_programs(2) - 1)
    def _():
        x4 = x_ref[...] + gt_ref[...] * acc[...]
        y_out[...] = _rms_mod(x4, fg_ref[...], fsh_ref[...], fsc_ref[...])


def _moe(h3, comb, x3, w1, w3, w2, gt, fg, fsh, fsc, tm_pref=1024):
    s_, r_, d = x3.shape
    n_exp, _, fe = w1.shape
    tm = _tile(r_, tm_pref, V7X_LANES)
    n_lane = comb.shape[-1]
    assert n_lane == V7X_LANES
    body = functools.partial(_moe_kernel, min(MOE_ROWS, tm), min(MOE_ROWS_TAIL, tm))
    rs = pl.BlockSpec((None, tm, d), lambda s, t, e: (s, t, 0))

    def ms(arr):
        if arr.shape[1] == 1:
            return pl.BlockSpec((None, 1, d), lambda s, t, e: (s, 0, 0))
        return rs
    return pl.pallas_call(
        body,
        out_shape=jax.ShapeDtypeStruct((s_, r_, d), F32),
        grid=(s_, r_ // tm, n_exp),
        in_specs=[rs, pl.BlockSpec((None, tm, n_lane), lambda s, t, e: (s, t, 0)), rs,
                  pl.BlockSpec((None, d, fe), lambda s, t, e: (e, 0, 0)),
                  pl.BlockSpec((None, d, fe), lambda s, t, e: (e, 0, 0)),
                  pl.BlockSpec((None, fe, d), lambda s, t, e: (e, 0, 0)),
                  ms(gt), pl.BlockSpec(fg.shape, lambda s, t, e: (0, 0)), ms(fsh), ms(fsc)],
        out_specs=rs,
        scratch_shapes=[pltpu.VMEM((tm, d), F32), pltpu.VMEM((tm, d), F32),
                        pltpu.VMEM((tm, n_lane), F32), pltpu.VMEM((tm, n_lane), F32),
                        pltpu.VMEM((n_lane, tm), F32), pltpu.VMEM((n_lane, tm), F32)],
        compiler_params=_cparams("arbitrary", "arbitrary", "arbitrary"),
        name="moe",
    )(h3, comb, x3, w1, w3, w2, gt, fg, fsh, fsc)


def _trunk(x, mods0, mods1, modf, wkv0, shift0, conv0, P):
    s_, r_, d = x.shape
    heads = d // HEAD_SIZE
    sh1, sc1, gt1, sh2, sc2, gt2 = mods0
    r, k, v, w, a, g, h_last = _pre0(x, P["nmix0"], sh1, sc1, shift0, P["rw"])
    if wkv0 is None:
        ops = [_to_lanes(t, heads) for t in (r, w, k, v, a)]
        z, s_fin = _scan(*ops, None, P["scan_consts"])
        z = _from_lanes(z, s_, heads)
        n_seq = s_
    else:
        ops = [_to_lanes(t.reshape(r_, 1, d), heads) for t in (r, w, k, v, a)]
        z, s_fin = _scan(*ops, wkv0, P["scan_consts"])
        z = _from_lanes(z, r_, heads).reshape(1, r_, d)
        n_seq = r_
    wkv_out = jnp.transpose(s_fin.reshape(HEAD_SIZE, HEAD_SIZE, n_seq, heads), (2, 3, 1, 0))
    x1, h2 = _post0(z, g, x, P["wo"], gt1, P["nffn0"], sh2, sc2)
    x2 = _ffn(h2, x1, P["ffn_w1"], P["ffn_w3"], P["ffn_w2"], gt2)
    sh1, sc1, gt1, sh2, sc2, gt2 = mods1
    u = _glu(x2, P["nmix1"], sh1, sc1, P["pw1"], P["pb1"])
    if conv0 is None:
        x3, h3, comb, conv_out = _conv_seq(u, x2, P["cv"], gt1, P["nffn1"], sh2, sc2,
                                           P["router_w"], P["router_b"])
    else:
        x3, h3, comb, conv_out = _conv_step(u, x2, conv0, P["cv"], gt1, P["nffn1"], sh2, sc2,
                                            P["router_w"], P["router_b"])
    fsh, fsc = modf
    y = _moe(h3, comb, x3, P["moe_w1"], P["moe_w3"], P["moe_w2"], gt2, P["final_g"], fsh, fsc)
    return y, wkv_out, h_last, conv_out


def kernel(x_prompt, x_sample, state_wkv, state_shift, state_conv, c_prompt, c_sample, ada_w, ada_b, norm_mix_g, norm_ffn_g, rw_mu, rw_wr, rw_wk, rw_wv, rw_wo, rw_w0, rw_w1, rw_w2, rw_a0, rw_a1, rw_a2, rw_g1, rw_g2, rw_kk, rw_ka, rw_rk, rw_lnx_g, rw_lnx_b, cv_pw1_w, cv_pw1_b, cv_dw_w, cv_dw_b, cv_ln_g, cv_ln_b, cv_pw2_w, cv_pw2_b, ffn_w1, ffn_w3, ffn_w2, moe_router_w, moe_router_b, moe_w1, moe_w3, moe_w2, final_g, final_ada_w, final_ada_b):
    nb, seq, d = x_prompt.shape
    db = x_sample.shape[0]
    heads = d // HEAD_SIZE
    n_exp = moe_router_w.shape[-1]
    assert (nb * heads) % V7X_LANES == 0 and (db * heads) % V7X_LANES == 0
    assert V7X_LANES % heads == 0 and x_sample.shape[1] == 1

    def row(vec):
        return vec.reshape(1, -1)

    def lane_tile(vec):
        t = vec.reshape(heads, HEAD_SIZE).T
        return jnp.tile(t, (1, V7X_LANES // heads))

    bf = lambda t: t.astype(BF16)
    P = {
        "nmix0": row(norm_mix_g[0]), "nffn0": row(norm_ffn_g[0]),
        "nmix1": row(norm_mix_g[1]), "nffn1": row(norm_ffn_g[1]),
        "rw": {"mu": rw_mu[0], "wr": bf(rw_wr[0]), "wk": bf(rw_wk[0]), "wv": bf(rw_wv[0]),
               "w1": bf(rw_w1[0]), "w2": bf(rw_w2[0]), "a1": bf(rw_a1[0]), "a2": bf(rw_a2[0]),
               "g1": bf(rw_g1[0]), "g2": bf(rw_g2[0]), "w0": row(rw_w0[0]), "a0": row(rw_a0[0])},
        "scan_consts": [lane_tile(rw_kk[0]), lane_tile(rw_ka[0]), lane_tile(rw_rk[0].reshape(-1)),
                        lane_tile(rw_lnx_g[0]), lane_tile(rw_lnx_b[0])],
        "wo": bf(rw_wo[0]),
        "ffn_w1": bf(ffn_w1[0]), "ffn_w3": bf(ffn_w3[0]), "ffn_w2": bf(ffn_w2[0]),
        "pw1": bf(cv_pw1_w[0]), "pb1": row(cv_pw1_b[0]),
        "cv": {"dw": cv_dw_w[0], "db": row(cv_dw_b[0]), "lng": row(cv_ln_g[0]),
               "lnb": row(cv_ln_b[0]), "pw2": bf(cv_pw2_w[0]), "pb2": row(cv_pw2_b[0])},
        "router_w": jnp.pad(moe_router_w[0], ((0, 0), (0, V7X_LANES - n_exp))),
        "router_b": jnp.pad(row(moe_router_b[0]), ((0, 0), (0, V7X_LANES - n_exp)),
                            constant_values=-jnp.inf),
        "moe_w1": bf(moe_w1[0]), "moe_w3": bf(moe_w3[0]), "moe_w2": bf(moe_w2[0]),
        "final_g": row(final_g),
    }

    c_all = jnp.concatenate([c_prompt, c_sample], axis=0)
    ada = [_ada(c_all, ada_w[i], ada_b[i]) for i in range(ada_w.shape[0])]
    ada_f = _ada(c_all, final_ada_w, final_ada_b)

    def mods(p, n, prompt):
        parts = jnp.split(p, n, axis=-1)
        if prompt:
            return [t[:nb].reshape(nb, 1, d) for t in parts]
        return [t[nb:].reshape(1, db, d) for t in parts]

    y_p, wkv_p, shift_p, conv_p = _trunk(
        x_prompt, mods(ada[0], 6, True), mods(ada[1], 6, True), mods(ada_f, 2, True),
        None, None, None, P)
    wkv0 = jnp.transpose(state_wkv[0], (3, 2, 0, 1)).reshape(HEAD_SIZE, HEAD_SIZE, db * heads)
    y_s, wkv_s, shift_s, conv_s = _trunk(
        x_sample.reshape(1, db, d), mods(ada[0], 6, False), mods(ada[1], 6, False),
        mods(ada_f, 2, False), wkv0, state_shift[0].reshape(1, db, d), state_conv[0], P)
    return (y_p, y_s.reshape(db, 1, d), wkv_p[None], shift_p.reshape(1, nb, d), conv_p[None],
            wkv_s[None], shift_s.reshape(1, db, d), conv_s[None])
```

```python
import functools

import jax
import jax.numpy as jnp
from jax import lax
from jax.experimental import pallas as pl
from jax.experimental.pallas import tpu as pltpu

F32 = jnp.float32
BF16 = jnp.bfloat16

HEAD_SIZE = 64
RMS_EPS = 1e-6
LN_EPS = 1e-5
GN_EPS = 64e-5
MIN_DECAY = 0.5452
SCAN_CHUNK = 64

V7X_LANES = 128
V7X_SUBLANES = 8
V7X_VMEM_BYTES = 64 * 1024 * 1024
VMEM_LIMIT_BYTES = V7X_VMEM_BYTES - 8 * 1024 * 1024


def _cparams(*sem):
    return pltpu.CompilerParams(dimension_semantics=sem, vmem_limit_bytes=VMEM_LIMIT_BYTES)


def _tile(n, pref, mult):
    if n <= pref:
        return n
    t = (pref // mult) * mult
    while t >= mult:
        if n % t == 0:
            return t
        t -= mult
    return n


def _bdot(a, b):
    return jnp.dot(a.astype(BF16), b.astype(BF16), preferred_element_type=F32)


def _split_bf16(x):
    hi = x.astype(BF16)
    lo = (x - hi.astype(F32)).astype(BF16)
    return hi, lo


def _dot3(a, b):
    a_hi, a_lo = _split_bf16(a)
    b_hi, b_lo = _split_bf16(b)
    d = functools.partial(jnp.dot, preferred_element_type=F32)
    return d(a_hi, b_hi) + d(a_hi, b_lo) + d(a_lo, b_hi)


def _rms_mod(x, g, shift, scale):
    ms = jnp.mean(x * x, axis=-1, keepdims=True)
    y = x * lax.rsqrt(ms + RMS_EPS) * g
    return y * (1.0 + scale) + shift


def _const_spec(shape):
    nd = len(shape)
    return pl.BlockSpec(shape, lambda *_: (0,) * nd)


def _ada_kernel(c_ref, w_ref, b_ref, o_ref):
    s = jax.nn.silu(c_ref[...])
    o_ref[...] = _dot3(s, w_ref[...]) + b_ref[...]


def _ada(c, w, b):
    rows, d = c.shape
    n = w.shape[1]
    tn = _tile(n, 1024, V7X_LANES)
    return pl.pallas_call(
        _ada_kernel,
        out_shape=jax.ShapeDtypeStruct((rows, n), F32),
        grid=(n // tn,),
        in_specs=[
            pl.BlockSpec((rows, d), lambda j: (0, 0)),
            pl.BlockSpec((d, tn), lambda j: (0, j)),
            pl.BlockSpec((1, tn), lambda j: (0, j)),
        ],
        out_specs=pl.BlockSpec((rows, tn), lambda j: (0, j)),
        compiler_params=_cparams("arbitrary"),
        name="ada",
    )(c, w, b.reshape(1, n))


def _row_spec(tm, d):
    return pl.BlockSpec((None, tm, d), lambda s, t: (s, t, 0))


def _mod_spec(arr, tm):
    d = arr.shape[-1]
    if arr.shape[1] == 1:
        return pl.BlockSpec((None, 1, d), lambda s, t: (s, 0, 0))
    return pl.BlockSpec((None, tm, d), lambda s, t: (s, t, 0))


def _w2_spec(shape):
    return pl.BlockSpec(shape, lambda s, t: (0, 0))


def _pre0_kernel(sequential, *refs):
    if sequential:
        (x_ref, g_ref, sh_ref, sc_ref, mu_ref, wr_ref, wk_ref, wv_ref, w1_ref, w2_ref, a1_ref,
         a2_ref, g1_ref, g2_ref, w0_ref, a0_ref,
         r_out, k_out, v_out, w_out, a_out, g_out, h_out, carry) = refs
        hprev_ref = None
    else:
        (x_ref, g_ref, sh_ref, sc_ref, hprev_ref, mu_ref, wr_ref, wk_ref, wv_ref, w1_ref, w2_ref,
         a1_ref, a2_ref, g1_ref, g2_ref, w0_ref, a0_ref,
         r_out, k_out, v_out, w_out, a_out, g_out, h_out) = refs
    h = _rms_mod(x_ref[...], g_ref[...], sh_ref[...], sc_ref[...])
    tm = h.shape[0]
    if sequential:
        @pl.when(pl.program_id(1) == 0)
        def _():
            carry[...] = jnp.zeros_like(carry)
        row = lax.broadcasted_iota(jnp.int32, h.shape, 0)
        h_prev = jnp.where(row == 0, carry[...], pltpu.roll(h, 1, 0))
        carry[...] = h[tm - 1:tm, :]
        h_out[...] = h[tm - 1:tm, :]
    else:
        h_prev = hprev_ref[...]
        h_out[...] = h
    xx = h_prev - h
    mu = mu_ref[...]
    xr, xw, xk, xv, xa, xg = [h + xx * mu[i:i + 1, :] for i in range(6)]
    r_out[...] = _bdot(xr, wr_ref[...])
    k_out[...] = _bdot(xk, wk_ref[...])
    v_out[...] = _bdot(xv, wv_ref[...])
    w_raw = w0_ref[...] + _bdot(jnp.tanh(_bdot(xw, w1_ref[...])), w2_ref[...])
    w_out[...] = jnp.exp(-jnp.exp(-0.5) * jax.nn.sigmoid(w_raw))
    a_out[...] = jax.nn.sigmoid(a0_ref[...] + _bdot(_bdot(xa, a1_ref[...]), a2_ref[...]))
    g_out[...] = _bdot(jax.nn.sigmoid(_bdot(xg, g1_ref[...])), g2_ref[...])


def _pre0(x, g, sh, sc, h_prev, p, tm_pref=256):
    s_, r_, d = x.shape
    tm = _tile(r_, tm_pref, V7X_SUBLANES)
    sequential = h_prev is None
    weights = [p["mu"], p["wr"], p["wk"], p["wv"], p["w1"], p["w2"], p["a1"], p["a2"], p["g1"],
               p["g2"], p["w0"], p["a0"]]
    ins = [x, g, sh, sc] + ([] if sequential else [h_prev]) + weights
    in_specs = [_row_spec(tm, d), _w2_spec(g.shape), _mod_spec(sh, tm), _mod_spec(sc, tm)]
    if not sequential:
        in_specs.append(_row_spec(tm, d))
    in_specs += [_w2_spec(w.shape) for w in weights]
    big = jax.ShapeDtypeStruct((s_, r_, d), F32)
    if sequential:
        h_shape = jax.ShapeDtypeStruct((s_, 1, d), F32)
        h_spec = pl.BlockSpec((None, 1, d), lambda s, t: (s, 0, 0))
        scratch = [pltpu.VMEM((1, d), F32)]
    else:
        h_shape, h_spec, scratch = big, _row_spec(tm, d), []
    return pl.pallas_call(
        functools.partial(_pre0_kernel, sequential),
        out_shape=[big] * 6 + [h_shape],
        grid=(s_, r_ // tm),
        in_specs=in_specs,
        out_specs=[_row_spec(tm, d)] * 6 + [h_spec],
        scratch_shapes=scratch,
        compiler_params=_cparams("arbitrary", "arbitrary"),
        name="pre0",
    )(*ins)


def _scan_kernel(zero_init, tc_len, *refs):
    if zero_init:
        (r_ref, w_ref, k_ref, v_ref, a_ref, kk_ref, ka_ref, rk_ref, lg_ref, lb_ref,
         z_ref, sT_ref, s_scr, ab_scr) = refs
        s0_ref = None
    else:
        (r_ref, w_ref, k_ref, v_ref, a_ref, s0_ref, kk_ref, ka_ref, rk_ref, lg_ref, lb_ref,
         z_ref, sT_ref, s_scr, ab_scr) = refs
    n = HEAD_SIZE

    @pl.when(pl.program_id(1) == 0)
    def _():
        if zero_init:
            s_scr[...] = jnp.zeros_like(s_scr)
        else:
            s_scr[...] = s0_ref[...]

    kk_t, ka_t, rk_t, lg_t, lb_t = kk_ref[...], ka_ref[...], rk_ref[...], lg_ref[...], lb_ref[...]

    def step(t, c_prev):
        k = k_ref[t]
        a = a_ref[t]
        r = r_ref[t]
        kk = k * kk_t
        nrm = jnp.sqrt(jnp.sum(kk * kk, axis=0, keepdims=True))
        kk = kk / jnp.maximum(nrm, 1e-12)
        k_eff = k * (1.0 + (a - 1.0) * ka_t)
        bonus_dot = jnp.sum(r * k_eff * rk_t, axis=0, keepdims=True)
        c_new = c_prev * w_ref[t]
        c_inv = 1.0 / c_new
        ab_scr[0] = -kk * c_prev
        ab_scr[1] = kk * a * c_inv
        ab_scr[2] = k_eff * c_inv
        ab_scr[3] = r * c_new
        v = v_ref[t]
        sa = s_scr[0] * ab_scr[0, 0:1, :]
        for j in range(1, n):
            sa = sa + s_scr[j] * ab_scr[0, j:j + 1, :]
        y = None
        for j in range(n):
            q_new = s_scr[j] + sa * ab_scr[1, j:j + 1, :] + v * ab_scr[2, j:j + 1, :]
            s_scr[j] = q_new
            term = q_new * ab_scr[3, j:j + 1, :]
            y = term if y is None else y + term
        mean = jnp.mean(y, axis=0, keepdims=True)
        dlt = y - mean
        var = jnp.mean(dlt * dlt, axis=0, keepdims=True)
        yn = dlt * lax.rsqrt(var + GN_EPS) * lg_t + lb_t
        z_ref[t] = yn + bonus_dot * v_ref[t]
        return c_new

    c_fin = lax.fori_loop(0, tc_len, step, jnp.ones((n, V7X_LANES), F32))
    ab_scr[0] = c_fin
    for j in range(n):
        s_scr[j] = s_scr[j] * ab_scr[0, j:j + 1, :]

    @pl.when(pl.program_id(1) == pl.num_programs(1) - 1)
    def _():
        sT_ref[...] = s_scr[...]


def _scan(r, w, k, v, a, s0, consts):
    t_len, n, inst = r.shape
    tc = _tile(t_len, SCAN_CHUNK, 1)
    assert MIN_DECAY ** tc > 1e-30
    groups = inst // V7X_LANES
    op_spec = pl.BlockSpec((tc, n, V7X_LANES), lambda g, c: (c, 0, g))
    st_spec = pl.BlockSpec((n, n, V7X_LANES), lambda g, c: (0, 0, g))
    c_spec = pl.BlockSpec((n, V7X_LANES), lambda g, c: (0, 0))
    zero_init = s0 is None
    ins = [r, w, k, v, a] + ([] if zero_init else [s0]) + list(consts)
    in_specs = [op_spec] * 5 + ([] if zero_init else [st_spec]) + [c_spec] * len(consts)
    return pl.pallas_call(
        functools.partial(_scan_kernel, zero_init, tc),
        out_shape=[jax.ShapeDtypeStruct((t_len, n, inst), F32),
                   jax.ShapeDtypeStruct((n, n, inst), F32)],
        grid=(groups, t_len // tc),
        in_specs=in_specs,
        out_specs=[op_spec, st_spec],
        scratch_shapes=[pltpu.VMEM((n, n, V7X_LANES), F32), pltpu.VMEM((4, n, V7X_LANES), F32)],
        compiler_params=_cparams("arbitrary", "arbitrary"),
        name="wkv_scan",
    )(*ins)


def _to_lanes(x, heads):
    s_, r_, d = x.shape
    return jnp.transpose(x.reshape(s_, r_, heads, HEAD_SIZE), (1, 3, 0, 2)).reshape(
        r_, HEAD_SIZE, s_ * heads)


def _from_lanes(z, s_, heads):
    t_len = z.shape[0]
    return jnp.transpose(z.reshape(t_len, HEAD_SIZE, s_, heads), (2, 0, 3, 1)).reshape(
        s_, t_len, heads * HEAD_SIZE)


def _post0_kernel(z_ref, g_ref, x_ref, wo_ref, gt_ref, ng_ref, sh_ref, sc_ref, x1_out, h2_out):
    o = _bdot(z_ref[...] * g_ref[...], wo_ref[...])
    x1 = x_ref[...] + gt_ref[...] * o
    x1_out[...] = x1
    h2_out[...] = _rms_mod(x1, ng_ref[...], sh_ref[...], sc_ref[...]).astype(BF16)


def _post0(z, g, x, wo, gt, ng, sh, sc, tm_pref=512):
    s_, r_, d = x.shape
    tm = _tile(r_, tm_pref, V7X_SUBLANES)
    rs = _row_spec(tm, d)
    return pl.pallas_call(
        _post0_kernel,
        out_shape=[jax.ShapeDtypeStruct((s_, r_, d), F32), jax.ShapeDtypeStruct((s_, r_, d), BF16)],
        grid=(s_, r_ // tm),
        in_specs=[rs, rs, rs, _w2_spec(wo.shape), _mod_spec(gt, tm), _w2_spec(ng.shape),
                  _mod_spec(sh, tm), _mod_spec(sc, tm)],
        out_specs=[rs, rs],
        compiler_params=_cparams("arbitrary", "arbitrary"),
        name="post0",
    )(z, g, x, wo, gt, ng, sh, sc)


def _ffn_kernel(h_ref, x_ref, w1_ref, w3_ref, w2_ref, gt_ref, x2_out, acc):
    f = pl.program_id(2)

    @pl.when(f == 0)
    def _():
        acc[...] = jnp.zeros_like(acc)
    h = h_ref[...]
    a = jax.nn.silu(jnp.dot(h, w1_ref[...], preferred_element_type=F32))
    b = jnp.dot(h, w3_ref[...], preferred_element_type=F32)
    acc[...] += _bdot(a * b, w2_ref[...])

    @pl.when(f == pl.num_programs(2) - 1)
    def _():
        x2_out[...] = x_ref[...] + gt_ref[...] * acc[...]


def _ffn(h2, x1, w1, w3, w2, gt, tm_pref=512, tf_pref=1408):
    s_, r_, d = x1.shape
    ff = w1.shape[1]
    tm = _tile(r_, tm_pref, V7X_SUBLANES)
    tf = _tile(ff, tf_pref, V7X_LANES)
    rs = pl.BlockSpec((None, tm, d), lambda s, t, f: (s, t, 0))
    if gt.shape[1] == 1:
        gs = pl.BlockSpec((None, 1, d), lambda s, t, f: (s, 0, 0))
    else:
        gs = rs
    return pl.pallas_call(
        _ffn_kernel,
        out_shape=jax.ShapeDtypeStruct((s_, r_, d), F32),
        grid=(s_, r_ // tm, ff // tf),
        in_specs=[rs, rs,
                  pl.BlockSpec((d, tf), lambda s, t, f: (0, f)),
                  pl.BlockSpec((d, tf), lambda s, t, f: (0, f)),
                  pl.BlockSpec((tf, d), lambda s, t, f: (f, 0)),
                  gs],
        out_specs=rs,
        scratch_shapes=[pltpu.VMEM((tm, d), F32)],
        compiler_params=_cparams("arbitrary", "arbitrary", "arbitrary"),
        name="ffn",
    )(h2, x1, w1, w3, w2, gt)


def _glu_kernel(x_ref, ng_ref, sh_ref, sc_ref, pw_ref, pb_ref, u_out):
    d = x_ref.shape[-1]
    h = _rms_mod(x_ref[...], ng_ref[...], sh_ref[...], sc_ref[...])
    u = _bdot(h, pw_ref[...]) + pb_ref[...]
    u_out[...] = u[:, :d] * jax.nn.sigmoid(u[:, d:])


def _glu(x, ng, sh, sc, pw, pb, tm_pref=512):
    s_, r_, d = x.shape
    tm = _tile(r_, tm_pref, V7X_SUBLANES)
    rs = _row_spec(tm, d)
    return pl.pallas_call(
        _glu_kernel,
        out_shape=jax.ShapeDtypeStruct((s_, r_, d), F32),
        grid=(s_, r_ // tm),
        in_specs=[rs, _w2_spec(ng.shape), _mod_spec(sh, tm), _mod_spec(sc, tm),
                  _w2_spec(pw.shape), _w2_spec(pb.shape)],
        out_specs=rs,
        compiler_params=_cparams("arbitrary", "arbitrary"),
        name="glu",
    )(x, ng, sh, sc, pw, pb)


def _conv_tail(z, x, lng, lnb, pw2, pb2, gt, ng, sh, sc, rw, rb):
    mean = jnp.mean(z, axis=-1, keepdims=True)
    dlt = z - mean
    var = jnp.mean(dlt * dlt, axis=-1, keepdims=True)
    zn = dlt * lax.rsqrt(var + LN_EPS) * lng + lnb
    o = _bdot(jax.nn.silu(zn), pw2) + pb2
    x3 = x + gt * o
    h3 = _rms_mod(x3, ng, sh, sc)
    logits = _dot3(h3, rw) + rb
    n_exp = rb.shape[-1]
    lane = lax.broadcasted_iota(jnp.int32, logits.shape, 1)
    neg = jnp.float32(-jnp.inf)
    lg = logits
    v1 = jnp.max(lg, axis=-1, keepdims=True)
    i1 = jnp.min(jnp.where(lg == v1, lane, n_exp), axis=-1, keepdims=True)
    lg2 = jnp.where(lane == i1, neg, lg)
    v2 = jnp.max(lg2, axis=-1, keepdims=True)
    i2 = jnp.min(jnp.where(lg2 == v2, lane, n_exp), axis=-1, keepdims=True)
    e2 = jnp.exp(v2 - v1)
    den = 1.0 + e2
    comb = jnp.where(lane == i1, 1.0 / den, 0.0) + jnp.where(lane == i2, e2 / den, 0.0)
    return x3, h3, comb


def _conv_seq_kernel(width, u_ref, x_ref, dw_ref, db_ref, lng_ref, lnb_ref, pw2_ref, pb2_ref,
                     gt_ref, ng_ref, sh_ref, sc_ref, rw_ref, rb_ref,
                     x3_out, h3_out, comb_out, cst_out, ext, shifted):
    tm = u_ref.shape[0]
    halo = ext.shape[0] - tm
    t = pl.program_id(1)

    @pl.when(t == 0)
    def _():
        ext[0:halo, :] = jnp.zeros((halo, ext.shape[1]), F32)

    @pl.when(t > 0)
    def _():
        ext[0:halo, :] = ext[tm:tm + halo, :]
    ext[halo:halo + tm, :] = u_ref[...]
    off = halo - (width - 1)
    z = None
    for res in range(V7X_SUBLANES):
        taps = [wi for wi in range(width) if (off + wi) % V7X_SUBLANES == res]
        if not taps:
            continue
        span = (off + taps[-1]) - res
        if res == 0:
            src = ext
        else:
            src = shifted.at[res - 1]
            src[0:span + tm, :] = ext[res:res + span + tm, :]
        for wi in taps:
            lo = off + wi - res
            term = src[lo:lo + tm, :] * dw_ref[wi:wi + 1, :]
            z = term if z is None else z + term
    z = z + db_ref[...]
    x3, h3, comb = _conv_tail(z, x_ref[...], lng_ref[...], lnb_ref[...], pw2_ref[...], pb2_ref[...],
                              gt_ref[...], ng_ref[...], sh_ref[...], sc_ref[...], rw_ref[...],
                              rb_ref[...])
    x3_out[...] = x3
    h3_out[...] = h3.astype(BF16)
    comb_out[...] = comb
    cst_out[...] = ext[halo + tm - (width - 1):halo + tm, :]


def _conv_seq(u, x, cw, gt, ng, sh, sc, rw, rb, tm_pref=512):
    s_, r_, d = x.shape
    width = cw["dw"].shape[0]
    tm = _tile(r_, tm_pref, V7X_SUBLANES)
    halo = -(-(width - 1) // V7X_SUBLANES) * V7X_SUBLANES
    assert tm >= halo and r_ >= width - 1
    rs = _row_spec(tm, d)
    ws = [cw["dw"], cw["db"], cw["lng"], cw["lnb"], cw["pw2"], cw["pb2"]]
    n_lane = rw.shape[1]
    return pl.pallas_call(
        functools.partial(_conv_seq_kernel, width),
        out_shape=[jax.ShapeDtypeStruct((s_, r_, d), F32), jax.ShapeDtypeStruct((s_, r_, d), BF16),
                   jax.ShapeDtypeStruct((s_, r_, n_lane), F32),
                   jax.ShapeDtypeStruct((s_, width - 1, d), F32)],
        grid=(s_, r_ // tm),
        in_specs=[rs, rs] + [_w2_spec(w.shape) for w in ws]
        + [_mod_spec(gt, tm), _w2_spec(ng.shape), _mod_spec(sh, tm), _mod_spec(sc, tm),
           _w2_spec(rw.shape), _w2_spec(rb.shape)],
        out_specs=[rs, rs, pl.BlockSpec((None, tm, n_lane), lambda s, t: (s, t, 0)),
                   pl.BlockSpec((None, width - 1, d), lambda s, t: (s, 0, 0))],
        scratch_shapes=[pltpu.VMEM((tm + halo, d), F32),
                        pltpu.VMEM((V7X_SUBLANES - 1, tm + halo, d), F32)],
        compiler_params=_cparams("arbitrary", "arbitrary"),
        name="conv_seq",
    )(u, x, *ws, gt, ng, sh, sc, rw, rb)


def _conv_step_kernel(width, u_ref, x_ref, buf_ref, dw_ref, db_ref, lng_ref, lnb_ref, pw2_ref,
                      pb2_ref, gt_ref, ng_ref, sh_ref, sc_ref, rw_ref, rb_ref,
                      x3_out, h3_out, comb_out, cst_out):
    u = u_ref[...]
    z = u * dw_ref[width - 1:width, :] + db_ref[...]
    for wi in range(width - 1):
        z = z + buf_ref[:, wi, :] * dw_ref[wi:wi + 1, :]
    x3, h3, comb = _conv_tail(z, x_ref[...], lng_ref[...], lnb_ref[...], pw2_ref[...], pb2_ref[...],
                              gt_ref[...], ng_ref[...], sh_ref[...], sc_ref[...], rw_ref[...],
                              rb_ref[...])
    x3_out[...] = x3
    h3_out[...] = h3.astype(BF16)
    comb_out[...] = comb
    for wi in range(width - 2):
        cst_out[:, wi, :] = buf_ref[:, wi + 1, :]
    cst_out[:, width - 2, :] = u


def _conv_step(u, x, buf, cw, gt, ng, sh, sc, rw, rb, tm_pref=32):
    s_, r_, d = x.shape
    width = cw["dw"].shape[0]
    tm = _tile(r_, tm_pref, V7X_SUBLANES)
    rs = _row_spec(tm, d)
    bs = pl.BlockSpec((tm, width - 1, d), lambda s, t: (t, 0, 0))
    ws = [cw["dw"], cw["db"], cw["lng"], cw["lnb"], cw["pw2"], cw["pb2"]]
    n_lane = rw.shape[1]
    return pl.pallas_call(
        functools.partial(_conv_step_kernel, width),
        out_shape=[jax.ShapeDtypeStruct((s_, r_, d), F32), jax.ShapeDtypeStruct((s_, r_, d), BF16),
                   jax.ShapeDtypeStruct((s_, r_, n_lane), F32),
                   jax.ShapeDtypeStruct((r_, width - 1, d), F32)],
        grid=(s_, r_ // tm),
        in_specs=[rs, rs, bs] + [_w2_spec(w.shape) for w in ws]
        + [_mod_spec(gt, tm), _w2_spec(ng.shape), _mod_spec(sh, tm), _mod_spec(sc, tm),
           _w2_spec(rw.shape), _w2_spec(rb.shape)],
        out_specs=[rs, rs, pl.BlockSpec((None, tm, n_lane), lambda s, t: (s, t, 0)), bs],
        compiler_params=_cparams("arbitrary", "arbitrary"),
        name="conv_step",
    )(u, x, buf, *ws, gt, ng, sh, sc, rw, rb)


MOE_ROWS = 256
MOE_ROWS_TAIL = 128


def _moe_kernel(rows_full, rows_tail, h_ref, comb_ref, x_ref, w1_ref, w3_ref, w2_ref, gt_ref, fg_ref,
                fsh_ref, fsc_ref, y_out, acc, sel_scr, rank_scr, sel_t_scr, rank_t_scr):
    e = pl.program_id(2)
    tm = h_ref.shape[0]

    @pl.when(e == 0)
    def _():
        acc[...] = jnp.zeros_like(acc)
        sel = (comb_ref[...] > 0.0).astype(F32)
        row_i = lax.broadcasted_iota(jnp.int32, (tm, tm), 0)
        col_i = lax.broadcasted_iota(jnp.int32, (tm, tm), 1)
        before = (col_i < row_i).astype(BF16)
        rank = jnp.dot(before, sel.astype(BF16), preferred_element_type=F32)
        sel_scr[...] = sel
        rank_scr[...] = rank
        sel_t_scr[...] = sel.T
        rank_t_scr[...] = rank.T

    lane = lax.broadcasted_iota(jnp.int32, (tm, V7X_LANES), 1)

    def column(a):
        return jnp.sum(jnp.where(lane == e, a, 0.0), axis=-1, keepdims=True)

    sel_c, rank_c, gate_c = column(sel_scr[...]), column(rank_scr[...]), column(comb_ref[...])
    sel_r = sel_t_scr[pl.ds(e, 1), :]
    rank_r = rank_t_scr[pl.ds(e, 1), :]
    count = jnp.sum(sel_r).astype(jnp.int32)

    def expert_pass(base, rows):
        basef = base.astype(F32)
        slot_r = lax.broadcasted_iota(jnp.int32, (rows, tm), 0).astype(F32) + basef
        gather = jnp.where((rank_r == slot_r) & (sel_r > 0.0), 1.0, 0.0).astype(BF16)
        xs = jnp.dot(gather, h_ref[...], preferred_element_type=F32).astype(BF16)
        a = jax.nn.silu(jnp.dot(xs, w1_ref[...], preferred_element_type=F32))
        b = jnp.dot(xs, w3_ref[...], preferred_element_type=F32)
        ys = _bdot(a * b, w2_ref[...])
        slot_c = lax.broadcasted_iota(jnp.int32, (tm, rows), 1).astype(F32) + basef
        scatter = jnp.where((rank_c == slot_c) & (sel_c > 0.0), 1.0, 0.0).astype(BF16)
        acc[...] += gate_c * jnp.dot(scatter, ys.astype(BF16), preferred_element_type=F32)

    n_full = count // rows_full

    def full_pass(i, carry):
        expert_pass(i * rows_full, rows_full)
        return carry
    lax.fori_loop(0, n_full, full_pass, 0)
    rem = count - n_full * rows_full

    @pl.when(rem > rows_tail)
    def _():
        expert_pass(n_full * rows_full, rows_full)

    @pl.when((rem > 0) & (rem <= rows_tail))
    def _():
        expert_pass(n_full * rows_full, rows_tail)

    @pl.when(e == pl.num_programs(2) - 1)
    def _():
        x4 = x_ref[...] + gt_ref[...] * acc[...]
        y_out[...] = _rms_mod(x4, fg_ref[...], fsh_ref[...], fsc_ref[...])


def _moe(h3, comb, x3, w1, w3, w2, gt, fg, fsh, fsc, tm_pref=1024):
    s_, r_, d = x3.shape
    n_exp, _, fe = w1.shape
    tm = _tile(r_, tm_pref, V7X_LANES)
    n_lane = comb.shape[-1]
    assert n_lane == V7X_LANES
    body = functools.partial(_moe_kernel, min(MOE_ROWS, tm), min(MOE_ROWS_TAIL, tm))
    rs = pl.BlockSpec((None, tm, d), lambda s, t, e: (s, t, 0))

    def ms(arr):
        if arr.shape[1] == 1:
            return pl.BlockSpec((None, 1, d), lambda s, t, e: (s, 0, 0))
        return rs
    return pl.pallas_call(
        body,
        out_shape=jax.ShapeDtypeStruct((s_, r_, d), F32),
        grid=(s_, r_ // tm, n_exp),
        in_specs=[rs, pl.BlockSpec((None, tm, n_lane), lambda s, t, e: (s, t, 0)), rs,
                  pl.BlockSpec((None, d, fe), lambda s, t, e: (e, 0, 0)),
                  pl.BlockSpec((None, d, fe), lambda s, t, e: (e, 0, 0)),
                  pl.BlockSpec((None, fe, d), lambda s, t, e: (e, 0, 0)),
                  ms(gt), pl.BlockSpec(fg.shape, lambda s, t, e: (0, 0)), ms(fsh), ms(fsc)],
        out_specs=rs,
        scratch_shapes=[pltpu.VMEM((tm, d), F32),
                        pltpu.VMEM((tm, n_lane), F32), pltpu.VMEM((tm, n_lane), F32),
                        pltpu.VMEM((n_lane, tm), F32), pltpu.VMEM((n_lane, tm), F32)],
        compiler_params=_cparams("arbitrary", "arbitrary", "arbitrary"),
        name="moe",
    )(h3, comb, x3, w1, w3, w2, gt, fg, fsh, fsc)


def _trunk(x, mods0, mods1, modf, wkv0, shift0, conv0, P):
    s_, r_, d = x.shape
    heads = d // HEAD_SIZE
    sh1, sc1, gt1, sh2, sc2, gt2 = mods0
    r, k, v, w, a, g, h_last = _pre0(x, P["nmix0"], sh1, sc1, shift0, P["rw"])
    if wkv0 is None:
        ops = [_to_lanes(t, heads) for t in (r, w, k, v, a)]
        z, s_fin = _scan(*ops, None, P["scan_consts"])
        z = _from_lanes(z, s_, heads)
        n_seq = s_
    else:
        ops = [_to_lanes(t.reshape(r_, 1, d), heads) for t in (r, w, k, v, a)]
        z, s_fin = _scan(*ops, wkv0, P["scan_consts"])
        z = _from_lanes(z, r_, heads).reshape(1, r_, d)
        n_seq = r_
    wkv_out = jnp.transpose(s_fin.reshape(HEAD_SIZE, HEAD_SIZE, n_seq, heads), (2, 3, 1, 0))
    x1, h2 = _post0(z, g, x, P["wo"], gt1, P["nffn0"], sh2, sc2)
    x2 = _ffn(h2, x1, P["ffn_w1"], P["ffn_w3"], P["ffn_w2"], gt2)
    sh1, sc1, gt1, sh2, sc2, gt2 = mods1
    u = _glu(x2, P["nmix1"], sh1, sc1, P["pw1"], P["pb1"])
    if conv0 is None:
        x3, h3, comb, conv_out = _conv_seq(u, x2, P["cv"], gt1, P["nffn1"], sh2, sc2,
                                           P["router_w"], P["router_b"])
    else:
        x3, h3, comb, conv_out = _conv_step(u, x2, conv0, P["cv"], gt1, P["nffn1"], sh2, sc2,
                                            P["router_w"], P["router_b"])
    fsh, fsc = modf
    y = _moe(h3, comb, x3, P["moe_w1"], P["moe_w3"], P["moe_w2"], gt2, P["final_g"], fsh, fsc)
    return y, wkv_out, h_last, conv_out


def kernel(x_prompt, x_sample, state_wkv, state_shift, state_conv, c_prompt, c_sample, ada_w, ada_b, norm_mix_g, norm_ffn_g, rw_mu, rw_wr, rw_wk, rw_wv, rw_wo, rw_w0, rw_w1, rw_w2, rw_a0, rw_a1, rw_a2, rw_g1, rw_g2, rw_kk, rw_ka, rw_rk, rw_lnx_g, rw_lnx_b, cv_pw1_w, cv_pw1_b, cv_dw_w, cv_dw_b, cv_ln_g, cv_ln_b, cv_pw2_w, cv_pw2_b, ffn_w1, ffn_w3, ffn_w2, moe_router_w, moe_router_b, moe_w1, moe_w3, moe_w2, final_g, final_ada_w, final_ada_b):
    nb, seq, d = x_prompt.shape
    db = x_sample.shape[0]
    heads = d // HEAD_SIZE
    n_exp = moe_router_w.shape[-1]
    assert (nb * heads) % V7X_LANES == 0 and (db * heads) % V7X_LANES == 0
    assert V7X_LANES % heads == 0 and x_sample.shape[1] == 1

    def row(vec):
        return vec.reshape(1, -1)

    def lane_tile(vec):
        t = vec.reshape(heads, HEAD_SIZE).T
        return jnp.tile(t, (1, V7X_LANES // heads))

    bf = lambda t: t.astype(BF16)
    P = {
        "nmix0": row(norm_mix_g[0]), "nffn0": row(norm_ffn_g[0]),
        "nmix1": row(norm_mix_g[1]), "nffn1": row(norm_ffn_g[1]),
        "rw": {"mu": rw_mu[0], "wr": bf(rw_wr[0]), "wk": bf(rw_wk[0]), "wv": bf(rw_wv[0]),
               "w1": bf(rw_w1[0]), "w2": bf(rw_w2[0]), "a1": bf(rw_a1[0]), "a2": bf(rw_a2[0]),
               "g1": bf(rw_g1[0]), "g2": bf(rw_g2[0]), "w0": row(rw_w0[0]), "a0": row(rw_a0[0])},
        "scan_consts": [lane_tile(rw_kk[0]), lane_tile(rw_ka[0]), lane_tile(rw_rk[0].reshape(-1)),
                        lane_tile(rw_lnx_g[0]), lane_tile(rw_lnx_b[0])],
        "wo": bf(rw_wo[0]),
        "ffn_w1": bf(ffn_w1[0]), "ffn_w3": bf(ffn_w3[0]), "ffn_w2": bf(ffn_w2[0]),
        "pw1": bf(cv_pw1_w[0]), "pb1": row(cv_pw1_b[0]),
        "cv": {"dw": cv_dw_w[0], "db": row(cv_dw_b[0]), "lng": row(cv_ln_g[0]),
               "lnb": row(cv_ln_b[0]), "pw2": bf(cv_pw2_w[0]), "pb2": row(cv_pw2_b[0])},
        "router_w": jnp.pad(moe_router_w[0], ((0, 0), (0, V7X_LANES - n_exp))),
        "router_b": jnp.pad(row(moe_router_b[0]), ((0, 0), (0, V7X_LANES - n_exp)),
                            constant_values=-jnp.inf),
        "moe_w1": bf(moe_w1[0]), "moe_w3": bf(moe_w3[0]), "moe_w2": bf(moe_w2[0]),
        "final_g": row(final_g),
    }

    c_all = jnp.concatenate([c_prompt, c_sample], axis=0)
    ada = [_ada(c_all, ada_w[i], ada_b[i]) for i in range(ada_w.shape[0])]
    ada_f = _ada(c_all, final_ada_w, final_ada_b)

    def mods(p, n, prompt):
        parts = jnp.split(p, n, axis=-1)
        if prompt:
            return [t[:nb].reshape(nb, 1, d) for t in parts]
        return [t[nb:].reshape(1, db, d) for t in parts]

    y_p, wkv_p, shift_p, conv_p = _trunk(
        x_prompt, mods(ada[0], 6, True), mods(ada[1], 6, True), mods(ada_f, 2, True),
        None, None, None, P)
    wkv0 = jnp.transpose(state_wkv[0], (3, 2, 0, 1)).reshape(HEAD_SIZE, HEAD_SIZE, db * heads)
    y_s, wkv_s, shift_s, conv_s = _trunk(
        x_sample.reshape(1, db, d), mods(ada[0], 6, False), mods(ada[1], 6, False),
        mods(ada_f, 2, False), wkv0, state_shift[0].reshape(1, db, d), state_conv[0], P)
    return (y_p, y_s.reshape(db, 1, d), wkv_p[None], shift_p.reshape(1, nb, d), conv_p[None],
            wkv_s[None], shift_s.reshape(1, db, d), conv_s[None])
```
